```python
import jax, jax.numpy as jnp
from jax import lax
import numpy as np

D_MODEL = 2048
BATCH = 8
SEQ = 2048
DEPTH = 1

MEM_LEN = 256
MLA_HEADS = 8
Q_LORA = 512
KV_LORA = 512
QK_NOPE = 128
QK_ROPE = 64
V_HEAD = 128
ROPE_THETA = 10000.0
MLA_OUT = MLA_HEADS * V_HEAD
CONV_WIDTH = 1024
CONV_K = 3
X_HEADS = 4
X_HEAD_DIM = 256
X_WIDTH = X_HEADS * X_HEAD_DIM
D_FF = 5632
FFN_CONV_K = 3
N_BRANCH = 3
Q_BLOCK = 128
EPS = 1e-6
SPLITS = (Q_LORA, KV_LORA, QK_ROPE, CONV_WIDTH, CONV_WIDTH, CONV_WIDTH, X_WIDTH, N_BRANCH * D_MODEL)
IN_COLS = Q_LORA + KV_LORA + QK_ROPE + 3 * CONV_WIDTH + X_WIDTH + N_BRANCH * D_MODEL

kernel_name = "hybrid_mla_shortconv_memxattn_convffn_encoder"


def rmsnorm(x, g):
    xf = x.astype(jnp.float32)
    y = xf * lax.rsqrt(jnp.mean(xf * xf, axis=-1, keepdims=True) + EPS)
    return (y * g.astype(jnp.float32)).astype(x.dtype)


def rope_angles(positions):
    inv_freq = jnp.power(ROPE_THETA, -jnp.arange(0, QK_ROPE, 2, dtype=jnp.float32) / QK_ROPE)
    ang = positions.astype(jnp.float32)[..., None] * inv_freq
    return jnp.cos(ang), jnp.sin(ang)


def apply_rope(t, cos, sin):
    tf = t.astype(jnp.float32)
    t1, t2 = jnp.split(tf, 2, axis=-1)
    out = jnp.concatenate([t1 * cos - t2 * sin, t1 * sin + t2 * cos], axis=-1)
    return out.astype(t.dtype)


def dwconv_centred(u, w):
    C = u.shape[-1]
    K = w.shape[0]
    return lax.conv_general_dilated(
        u, w[:, None, :].astype(u.dtype), window_strides=(1,), padding=[(K // 2, K // 2)],
        dimension_numbers=("NWC", "WIO", "NWC"), feature_group_count=C)


def mla_attention(q_nope, q_rope, k_nope, k_rope, v):
    B, S, H, _ = q_nope.shape
    nb = S // Q_BLOCK
    scale = (QK_NOPE + QK_ROPE) ** -0.5

    def blocks(t):
        return jnp.swapaxes(t.reshape((B, nb, Q_BLOCK) + t.shape[2:]), 0, 1)

    def one_block(qs):
        qn, qr = qs
        s = (jnp.einsum("bqhd,bkhd->bhqk", qn, k_nope)
             + jnp.einsum("bqhr,bkr->bhqk", qr, k_rope))
        p = jax.nn.softmax(s.astype(jnp.float32) * scale, axis=-1).astype(v.dtype)
        return jnp.einsum("bhqk,bkhd->bqhd", p, v)

    o = lax.map(one_block, (blocks(q_nope), blocks(q_rope)))
    return jnp.swapaxes(o, 0, 1).reshape(B, S, H * V_HEAD)


def memory_attention(q, mem_n, w_mem_kv):
    B, S, _ = q.shape
    M = mem_n.shape[1]
    kv = (mem_n @ w_mem_kv).reshape(B, M, 2, X_HEADS, X_HEAD_DIM)
    k, v = kv[:, :, 0], kv[:, :, 1]
    qh = q.reshape(B, S, X_HEADS, X_HEAD_DIM)
    s = jnp.einsum("bshd,bmhd->bhsm", qh, k).astype(jnp.float32) * (X_HEAD_DIM ** -0.5)
    p = jax.nn.softmax(s, axis=-1).astype(v.dtype)
    return jnp.einsum("bhsm,bmhd->bshd", p, v).reshape(B, S, X_WIDTH)


def hybrid_layer(x, mem, cos, sin, mix_norm, w_in, q_norm, w_uq, kv_norm, w_ukv, w_o_mla,
                 conv_w, w_out_conv, mem_norm, w_mem_kv, w_o_mem, gate_bias, w_o,
                 ffn_norm, w_up, ffn_conv_w, w_down):
    B, S, D = x.shape
    h = rmsnorm(x, mix_norm)
    z = h @ w_in
    idx = np.cumsum(np.array(SPLITS))[:-1].tolist()
    c_q, c_kv, k_r, cv, cb, cc, q_x, g = jnp.split(z, idx, axis=-1)

    q = (rmsnorm(c_q, q_norm) @ w_uq).reshape(B, S, MLA_HEADS, QK_NOPE + QK_ROPE)
    q_nope, q_rope = q[..., :QK_NOPE], q[..., QK_NOPE:]
    q_rope = apply_rope(q_rope, cos[:, :, None, :], sin[:, :, None, :])
    kv = (rmsnorm(c_kv, kv_norm) @ w_ukv).reshape(B, S, MLA_HEADS, QK_NOPE + V_HEAD)
    k_nope, v = kv[..., :QK_NOPE], kv[..., QK_NOPE:]
    k_rope = apply_rope(k_r, cos, sin)
    y_mla = mla_attention(q_nope, q_rope, k_nope, k_rope, v) @ w_o_mla

    y_conv = (cb * dwconv_centred(cc * cv, conv_w)) @ w_out_conv

    y_mem = memory_attention(q_x, rmsnorm(mem, mem_norm), w_mem_kv) @ w_o_mem

    gates = jax.nn.sigmoid((g + gate_bias).astype(jnp.float32)).astype(x.dtype)
    gates = gates.reshape(B, S, N_BRANCH, D)
    merged = gates[:, :, 0] * y_mla + gates[:, :, 1] * y_conv + gates[:, :, 2] * y_mem
    x = x + merged @ w_o

    u = dwconv_centred(rmsnorm(x, ffn_norm) @ w_up, ffn_conv_w)
    a, b = jnp.split(u, 2, axis=-1)
    x = x + (jax.nn.silu(a) * b) @ w_down
    return x


def setup_inputs(seed: int = 0) -> dict:
    key = jax.random.key(seed)
    ks = jax.random.split(key, 24)
    L = DEPTH
    f32 = jnp.float32

    def nrm(k, shape, scale):
        return jax.random.normal(k, shape, f32) * scale

    def gain(k, shape):
        return 1.0 + 0.01 * jax.random.normal(k, shape, f32)

    return {
        "x": nrm(ks[0], (BATCH, SEQ, D_MODEL), 1.0),
        "mem": nrm(ks[1], (BATCH, MEM_LEN, D_MODEL), 1.0),
        "positions": jnp.broadcast_to(jnp.arange(SEQ, dtype=jnp.int32), (BATCH, SEQ)),
        "mix_norm": gain(ks[2], (L, D_MODEL)),
        "w_in": nrm(ks[3], (L, D_MODEL, IN_COLS), D_MODEL ** -0.5),
        "q_norm": gain(ks[4], (L, Q_LORA)),
        "w_uq": nrm(ks[5], (L, Q_LORA, MLA_HEADS * (QK_NOPE + QK_ROPE)), Q_LORA ** -0.5),
        "kv_norm": gain(ks[6], (L, KV_LORA)),
        "w_ukv": nrm(ks[7], (L, KV_LORA, MLA_HEADS * (QK_NOPE + V_HEAD)), KV_LORA ** -0.5),
        "w_o_mla": nrm(ks[8], (L, MLA_OUT, D_MODEL), MLA_OUT ** -0.5),
        "conv_w": nrm(ks[9], (L, CONV_K, CONV_WIDTH), CONV_K ** -0.5),
        "w_out_conv": nrm(ks[10], (L, CONV_WIDTH, D_MODEL), CONV_WIDTH ** -0.5),
        "mem_norm": gain(ks[11], (L, D_MODEL)),
        "w_mem_kv": nrm(ks[12], (L, D_MODEL, 2 * X_WIDTH), D_MODEL ** -0.5),
        "w_o_mem": nrm(ks[13], (L, X_WIDTH, D_MODEL), X_WIDTH ** -0.5),
        "gate_bias": nrm(ks[14], (L, N_BRANCH * D_MODEL), 0.01),
        "w_o": nrm(ks[15], (L, D_MODEL, D_MODEL), D_MODEL ** -0.5),
        "ffn_norm": gain(ks[16], (L, D_MODEL)),
        "w_up": nrm(ks[17], (L, D_MODEL, 2 * D_FF), D_MODEL ** -0.5),
        "ffn_conv_w": nrm(ks[18], (L, FFN_CONV_K, 2 * D_FF), FFN_CONV_K ** -0.5),
        "w_down": nrm(ks[19], (L, D_FF, D_MODEL), D_FF ** -0.5),
        "final_norm": gain(ks[20], (D_MODEL,)),
    }


def reference(x, mem, positions, mix_norm, w_in, q_norm, w_uq, kv_norm, w_ukv, w_o_mla,
              conv_w, w_out_conv, mem_norm, w_mem_kv, w_o_mem, gate_bias, w_o,
              ffn_norm, w_up, ffn_conv_w, w_down, final_norm):
    cos, sin = rope_angles(positions)
    for l in range(DEPTH):
        x = hybrid_layer(x, mem, cos, sin, mix_norm[l], w_in[l], q_norm[l], w_uq[l], kv_norm[l],
                         w_ukv[l], w_o_mla[l], conv_w[l], w_out_conv[l], mem_norm[l],
                         w_mem_kv[l], w_o_mem[l], gate_bias[l], w_o[l], ffn_norm[l], w_up[l],
                         ffn_conv_w[l], w_down[l])
    return rmsnorm(x, final_norm)
```

```python
import functools

import jax
import jax.numpy as jnp
from jax import lax
from jax.experimental import pallas as pl
from jax.experimental.pallas import tpu as pltpu

F32 = jnp.float32
BF16 = jnp.bfloat16

D_MODEL = 2048
MLA_HEADS = 8
Q_LORA = 512
KV_LORA = 512
QK_NOPE = 128
QK_ROPE = 64
V_HEAD = 128
ROPE_THETA = 10000.0
CONV_WIDTH = 1024
X_HEADS = 4
X_HEAD_DIM = 256
X_WIDTH = X_HEADS * X_HEAD_DIM
D_FF = 5632
N_BRANCH = 3
EPS = 1e-6

LANES = 128
SUBLANES_F32 = 8
SUBLANES_BF16 = 16
HEAD_PAD = 256
VMEM_LIMIT = 56 * 1024 * 1024

Z_CQ = 0
Z_CKV = Q_LORA
Z_CV = Q_LORA + KV_LORA
Z_CB = Z_CV + CONV_WIDTH
Z_CC = Z_CB + CONV_WIDTH
Z_QX = Z_CC + CONV_WIDTH
Z_G = Z_QX + X_WIDTH
Z_COLS = Z_G + N_BRANCH * D_MODEL


def _params(*sem):
    return pltpu.CompilerParams(dimension_semantics=sem, vmem_limit_bytes=VMEM_LIMIT)


def _rms(x, g):
    inv = lax.rsqrt(jnp.mean(x * x, axis=-1, keepdims=True) + EPS)
    return x * inv * g


def _rms_rows_to(x_ref, g_ref, out_ref, rows, chunk, out_row0=0):
    def body(c, carry):
        r0 = pl.multiple_of(c * chunk, chunk)
        x = x_ref[pl.ds(r0, chunk), :].astype(F32)
        out_ref[pl.ds(out_row0 + r0, chunk), :] = _rms(x, g_ref[...]).astype(out_ref.dtype)
        return carry
    lax.fori_loop(0, rows // chunk, body, 0)


def _in_proj_kernel(x_ref, g_ref, w_ref, wkr_ref, z_ref, kr_ref, h_ref, *, tm):
    @pl.when(pl.program_id(1) == 0)
    def _():
        _rms_rows_to(x_ref, g_ref, h_ref, tm, 128)
        kr_ref[...] = jnp.dot(h_ref[...], wkr_ref[...], preferred_element_type=F32)

    z_ref[...] = jnp.dot(h_ref[...], w_ref[...], preferred_element_type=F32).astype(BF16)


def _in_proj(x2, g, w_main, w_kr, *, tm=1024, tn=1024):
    T = x2.shape[0]
    return pl.pallas_call(
        functools.partial(_in_proj_kernel, tm=tm),
        grid=(T // tm, Z_COLS // tn),
        in_specs=[
            pl.BlockSpec((tm, D_MODEL), lambda i, j: (i, 0)),
            pl.BlockSpec((1, D_MODEL), lambda i, j: (0, 0)),
            pl.BlockSpec((D_MODEL, tn), lambda i, j: (0, j)),
            pl.BlockSpec((D_MODEL, LANES), lambda i, j: (0, 0)),
        ],
        out_specs=[
            pl.BlockSpec((tm, tn), lambda i, j: (i, j)),
            pl.BlockSpec((tm, LANES), lambda i, j: (i, 0)),
        ],
        out_shape=[
            jax.ShapeDtypeStruct((T, Z_COLS), BF16),
            jax.ShapeDtypeStruct((T, LANES), F32),
        ],
        scratch_shapes=[pltpu.VMEM((tm, D_MODEL), BF16)],
        compiler_params=_params("parallel", "arbitrary"),
        name="in_proj",
    )(x2, g, w_main, w_kr)


def _qkv_kernel(cq_ref, ckv_ref, kr_ref, pos_ref, invf_ref, qn_ref, kvn_ref, wuq_ref, wukv_ref,
                q_ref, k_ref, v_ref, *, tm):
    ang = pos_ref[...].astype(F32) * invf_ref[...]
    cos = jnp.cos(ang)
    sin = jnp.sin(ang)
    lane = lax.broadcasted_iota(jnp.int32, (tm, LANES), 1)
    half = QK_ROPE // 2
    c_tab = jnp.where(lane < QK_ROPE, cos, 0.0)
    s_lo = jnp.where(lane < half, -sin, 0.0)
    s_hi = jnp.where((lane >= half) & (lane < QK_ROPE), sin, 0.0)

    def rope(t):
        return t * c_tab + pltpu.roll(t, LANES - half, 1) * s_lo + pltpu.roll(t, half, 1) * s_hi

    scale = (QK_NOPE + QK_ROPE) ** -0.5
    cqn = _rms(cq_ref[...].astype(F32), qn_ref[...]).astype(BF16)
    q = jnp.dot(cqn, wuq_ref[...], preferred_element_type=F32)
    nope_cols = MLA_HEADS * QK_NOPE
    for h in range(MLA_HEADS):
        q_ref[:, h * HEAD_PAD:h * HEAD_PAD + QK_NOPE] = (
            q[:, h * QK_NOPE:(h + 1) * QK_NOPE] * scale).astype(BF16)
        q_ref[:, h * HEAD_PAD + QK_NOPE:(h + 1) * HEAD_PAD] = (
            rope(q[:, nope_cols + h * LANES:nope_cols + (h + 1) * LANES]) * scale).astype(BF16)

    ckvn = _rms(ckv_ref[...].astype(F32), kvn_ref[...]).astype(BF16)
    kv = jnp.dot(ckvn, wukv_ref[...], preferred_element_type=F32)
    k_rope = rope(kr_ref[...]).astype(BF16)
    for h in range(MLA_HEADS):
        k_ref[:, h * HEAD_PAD:h * HEAD_PAD + QK_NOPE] = kv[:, h * HEAD_PAD:h * HEAD_PAD + QK_NOPE].astype(BF16)
        k_ref[:, h * HEAD_PAD + QK_NOPE:(h + 1) * HEAD_PAD] = k_rope
        v_ref[:, h * V_HEAD:(h + 1) * V_HEAD] = kv[:, h * HEAD_PAD + QK_NOPE:(h + 1) * HEAD_PAD].astype(BF16)


def _qkv(z, kr, pos, invf, q_norm, kv_norm, w_uq_p, w_ukv, *, tm=512):
    T = z.shape[0]
    qk_cols = MLA_HEADS * HEAD_PAD
    return pl.pallas_call(
        functools.partial(_qkv_kernel, tm=tm),
        grid=(T // tm,),
        in_specs=[
            pl.BlockSpec((tm, Q_LORA), lambda i: (i, Z_CQ // Q_LORA)),
            pl.BlockSpec((tm, KV_LORA), lambda i: (i, Z_CKV // KV_LORA)),
            pl.BlockSpec((tm, LANES), lambda i: (i, 0)),
            pl.BlockSpec((tm, 1), lambda i: (i, 0)),
            pl.BlockSpec((1, LANES), lambda i: (0, 0)),
            pl.BlockSpec((1, Q_LORA), lambda i: (0, 0)),
            pl.BlockSpec((1, KV_LORA), lambda i: (0, 0)),
            pl.BlockSpec((Q_LORA, qk_cols), lambda i: (0, 0)),
            pl.BlockSpec((KV_LORA, qk_cols), lambda i: (0, 0)),
        ],
        out_specs=[
            pl.BlockSpec((tm, qk_cols), lambda i: (i, 0)),
            pl.BlockSpec((tm, qk_cols), lambda i: (i, 0)),
            pl.BlockSpec((tm, MLA_HEADS * V_HEAD), lambda i: (i, 0)),
        ],
        out_shape=[
            jax.ShapeDtypeStruct((T, qk_cols), BF16),
            jax.ShapeDtypeStruct((T, qk_cols), BF16),
            jax.ShapeDtypeStruct((T, MLA_HEADS * V_HEAD), BF16),
        ],
        compiler_params=_params("parallel"),
        name="qkv",
    )(z, z, kr, pos, invf, q_norm, kv_norm, w_uq_p, w_ukv)


def _mla_kernel(q_ref, k_ref, v_ref, o_ref, *, seq, tq):
    def body(c, carry):
        r = pl.ds(pl.multiple_of(c * tq, tq), tq)
        s = lax.dot_general(q_ref[r, :], k_ref[...], (((1,), (1,)), ((), ())),
                            preferred_element_type=F32)
        m = jnp.max(s, axis=-1, keepdims=True)
        p = jnp.exp(s - m)
        l = jnp.sum(p, axis=-1, keepdims=True)
        o = jnp.dot(p.astype(BF16), v_ref[...], preferred_element_type=F32)
        o_ref[r, :] = (o / l).astype(BF16)
        return carry
    lax.fori_loop(0, seq // tq, body, 0)


def _mla(q, k, v, *, batch, seq, tq=512):
    T = q.shape[0]
    return pl.pallas_call(
        functools.partial(_mla_kernel, seq=seq, tq=tq),
        grid=(batch, MLA_HEADS),
        in_specs=[
            pl.BlockSpec((seq, HEAD_PAD), lambda b, h: (b, h)),
            pl.BlockSpec((seq, HEAD_PAD), lambda b, h: (b, h)),
            pl.BlockSpec((seq, V_HEAD), lambda b, h: (b, h)),
        ],
        out_specs=pl.BlockSpec((seq, V_HEAD), lambda b, h: (b, h)),
        out_shape=jax.ShapeDtypeStruct((T, MLA_HEADS * V_HEAD), BF16),
        compiler_params=_params("parallel", "parallel"),
        name="mla_attn",
    )(q, k, v)


def _mem_kv_kernel(m_ref, g_ref, w_ref, o_ref, h_ref, *, tm):
    @pl.when(pl.program_id(1) == 0)
    def _():
        _rms_rows_to(m_ref, g_ref, h_ref, tm, 128)

    o_ref[...] = jnp.dot(h_ref[...], w_ref[...], preferred_element_type=F32).astype(BF16)


def _mem_kv(mem2, g, w, *, tm=1024, tn=1024):
    R = mem2.shape[0]
    N = w.shape[1]
    return pl.pallas_call(
        functools.partial(_mem_kv_kernel, tm=tm),
        grid=(R // tm, N // tn),
        in_specs=[
            pl.BlockSpec((tm, D_MODEL), lambda i, j: (i, 0)),
            pl.BlockSpec((1, D_MODEL), lambda i, j: (0, 0)),
            pl.BlockSpec((D_MODEL, tn), lambda i, j: (0, j)),
        ],
        out_specs=pl.BlockSpec((tm, tn), lambda i, j: (i, j)),
        out_shape=jax.ShapeDtypeStruct((R, N), BF16),
        scratch_shapes=[pltpu.VMEM((tm, D_MODEL), BF16)],
        compiler_params=_params("parallel", "arbitrary"),
        name="mem_kv",
    )(mem2, g, w)


def _mem_attn_kernel(q_ref, kv_ref, o_ref):
    scale = X_HEAD_DIM ** -0.5
    for h in range(X_HEADS):
        c = slice(h * X_HEAD_DIM, (h + 1) * X_HEAD_DIM)
        k = kv_ref[:, h * X_HEAD_DIM:(h + 1) * X_HEAD_DIM]
        v = kv_ref[:, X_WIDTH + h * X_HEAD_DIM:X_WIDTH + (h + 1) * X_HEAD_DIM]
        s = lax.dot_general(q_ref[:, c], k, (((1,), (1,)), ((), ())), preferred_element_type=F32) * scale
        m = jnp.max(s, axis=-1, keepdims=True)
        p = jnp.exp(s - m)
        l = jnp.sum(p, axis=-1, keepdims=True)
        o = jnp.dot(p.astype(BF16), v, preferred_element_type=F32)
        o_ref[:, c] = (o / l).astype(BF16)


def _mem_attn(z, kvm, *, seq, mem_len, tm=1024):
    T = z.shape[0]
    return pl.pallas_call(
        _mem_attn_kernel,
        grid=(T // tm,),
        in_specs=[
            pl.BlockSpec((tm, X_WIDTH), lambda i: (i, Z_QX // X_WIDTH)),
            pl.BlockSpec((mem_len, 2 * X_WIDTH), lambda i: (i // (seq // tm), 0)),
        ],
        out_specs=pl.BlockSpec((tm, X_WIDTH), lambda i: (i, 0)),
        out_shape=jax.ShapeDtypeStruct((T, X_WIDTH), BF16),
        compiler_params=_params("parallel"),
        name="mem_attn",
    )(z, kvm)


def _merge_kernel(o_ref, cv_ref, cb_ref, cc_ref, cvp_ref, ccp_ref, cvn_ref, ccn_ref, yx_ref,
                  g0_ref, g1_ref, g2_ref, b0_ref, b1_ref, b2_ref, cw_ref,
                  wa_ref, wc_ref, wm_ref, out_ref, pext_ref, cbr_ref, *, tm, tiles_per_seq):
    i = pl.program_id(0)

    @pl.when(pl.program_id(1) == 0)
    def _():
        pos_in_seq = i % tiles_per_seq
        keep_prev = jnp.where(pos_in_seq == 0, 0.0, 1.0)
        keep_next = jnp.where(pos_in_seq == tiles_per_seq - 1, 0.0, 1.0)
        h8 = SUBLANES_F32
        prev = (cvp_ref[h8:, :].astype(F32) * ccp_ref[h8:, :].astype(F32)) * keep_prev
        nxt = (cvn_ref[:h8, :].astype(F32) * ccn_ref[:h8, :].astype(F32)) * keep_next
        pext_ref[0:h8, :] = prev
        pext_ref[tm + h8:tm + 2 * h8, :] = nxt
        pext_ref[h8:tm + h8, :] = cv_ref[...].astype(F32) * cc_ref[...].astype(F32)
        w0 = cw_ref[0:1, :]
        w1 = cw_ref[1:2, :]
        w2 = cw_ref[2:3, :]
        conv = (w0 * pext_ref[h8 - 1:tm + h8 - 1, :] + w1 * pext_ref[h8:tm + h8, :]
                + w2 * pext_ref[h8 + 1:tm + h8 + 1, :])
        cbr_ref[...] = (cb_ref[...].astype(F32) * conv).astype(BF16)

    def gate(g_ref, b_ref):
        return jax.nn.sigmoid(g_ref[...].astype(F32) + b_ref[...])

    y = gate(g0_ref, b0_ref) * jnp.dot(o_ref[...], wa_ref[...], preferred_element_type=F32)
    y += gate(g1_ref, b1_ref) * jnp.dot(cbr_ref[...], wc_ref[...], preferred_element_type=F32)
    y += gate(g2_ref, b2_ref) * jnp.dot(yx_ref[...], wm_ref[...], preferred_element_type=F32)
    out_ref[...] = y.astype(BF16)


def _merge(o_mla, z, yx, gate_bias, conv_w, w_o_mla, w_out_conv, w_o_mem, *, seq, tm=512, tn=1024):
    T = z.shape[0]
    hb = SUBLANES_BF16
    rb = tm // hb
    last_hb = T // hb - 1
    cw = CONV_WIDTH

    def zcol(c0, width):
        return c0 // width

    def gate_spec(k):
        return pl.BlockSpec((tm, tn), lambda i, j, k=k: (i, (Z_G + k * D_MODEL) // tn + j))

    def bias_spec(k):
        return pl.BlockSpec((1, tn), lambda i, j, k=k: (0, k * D_MODEL // tn + j))

    def prev_spec(c0):
        return pl.BlockSpec((hb, cw), lambda i, j: (jnp.maximum(i * rb - 1, 0), zcol(c0, cw)))

    def next_spec(c0):
        return pl.BlockSpec((hb, cw), lambda i, j: (jnp.minimum((i + 1) * rb, last_hb), zcol(c0, cw)))

    def w_spec():
        return pl.BlockSpec((cw, tn), lambda i, j: (0, j))

    return pl.pallas_call(
        functools.partial(_merge_kernel, tm=tm, tiles_per_seq=seq // tm),
        grid=(T // tm, D_MODEL // tn),
        in_specs=[
            pl.BlockSpec((tm, MLA_HEADS * V_HEAD), lambda i, j: (i, 0)),
            pl.BlockSpec((tm, cw), lambda i, j: (i, zcol(Z_CV, cw))),
            pl.BlockSpec((tm, cw), lambda i, j: (i, zcol(Z_CB, cw))),
            pl.BlockSpec((tm, cw), lambda i, j: (i, zcol(Z_CC, cw))),
            prev_spec(Z_CV), prev_spec(Z_CC), next_spec(Z_CV), next_spec(Z_CC),
            pl.BlockSpec((tm, X_WIDTH), lambda i, j: (i, 0)),
            gate_spec(0), gate_spec(1), gate_spec(2),
            bias_spec(0), bias_spec(1), bias_spec(2),
            pl.BlockSpec((3, cw), lambda i, j: (0, 0)),
            w_spec(), w_spec(), w_spec(),
        ],
        out_specs=pl.BlockSpec((tm, tn), lambda i, j: (i, j)),
        out_shape=jax.ShapeDtypeStruct((T, D_MODEL), BF16),
        scratch_shapes=[
            pltpu.VMEM((tm + 2 * SUBLANES_F32, cw), F32),
            pltpu.VMEM((tm, cw), BF16),
        ],
        compiler_params=_params("parallel", "arbitrary"),
        name="merge",
    )(o_mla, z, z, z, z, z, z, z, yx, z, z, z, gate_bias, gate_bias, gate_bias, conv_w,
      w_o_mla, w_out_conv, w_o_mem)


def _out_proj_kernel(m_ref, w_ref, x_ref, o_ref):
    o_ref[...] = x_ref[...] + jnp.dot(m_ref[...], w_ref[...], preferred_element_type=F32)


def _out_proj(merged, w_o, x2, *, tm=1024, tn=1024):
    T = x2.shape[0]
    return pl.pallas_call(
        _out_proj_kernel,
        grid=(T // tm, D_MODEL // tn),
        in_specs=[
            pl.BlockSpec((tm, D_MODEL), lambda i, j: (i, 0)),
            pl.BlockSpec((D_MODEL, tn), lambda i, j: (0, j)),
            pl.BlockSpec((tm, tn), lambda i, j: (i, j)),
        ],
        out_specs=pl.BlockSpec((tm, tn), lambda i, j: (i, j)),
        out_shape=jax.ShapeDtypeStruct((T, D_MODEL), F32),
        compiler_params=_params("parallel", "arbitrary"),
        name="out_proj",
    )(merged, w_o, x2)


def _ffn_kernel(x_ref, xp_ref, xn_ref, g_ref, wa_ref, wb_ref, ca_ref, cb_ref, wd_ref, fg_ref,
                o_ref, hn_ref, ua_ref, ub_ref, act_ref, acc_ref, *, tm, tf, tiles_per_seq, n_f, final):
    i = pl.program_id(0)
    f = pl.program_id(1)
    h8 = SUBLANES_F32

    @pl.when(f == 0)
    def _():
        pos_in_seq = i % tiles_per_seq
        keep_prev = jnp.where(pos_in_seq == 0, 0.0, 1.0)
        keep_next = jnp.where(pos_in_seq == tiles_per_seq - 1, 0.0, 1.0)
        halo = jnp.concatenate([_rms(xp_ref[...], g_ref[...]) * keep_prev,
                                _rms(xn_ref[...], g_ref[...]) * keep_next], axis=0)
        hn_ref[tm:tm + 2 * h8, :] = halo.astype(BF16)
        _rms_rows_to(x_ref, g_ref, hn_ref, tm, 128)

    def up(w_ref, u_ref):
        r = jnp.dot(hn_ref[...], w_ref[...], preferred_element_type=F32)
        u_ref[h8:tm + h8, :] = r[:tm]
        u_ref[0:h8, :] = r[tm:tm + h8]
        u_ref[tm + h8:tm + 2 * h8, :] = r[tm + h8:]

    up(wa_ref, ua_ref)
    up(wb_ref, ub_ref)

    def conv(u_ref, c_ref, r0, rows):
        return (c_ref[0:1, :] * u_ref[r0 + h8 - 1:r0 + h8 - 1 + rows, :]
                + c_ref[1:2, :] * u_ref[r0 + h8:r0 + h8 + rows, :]
                + c_ref[2:3, :] * u_ref[r0 + h8 + 1:r0 + h8 + 1 + rows, :])

    rows = 128
    for c in range(tm // rows):
        a = conv(ua_ref, ca_ref, c * rows, rows)
        b = conv(ub_ref, cb_ref, c * rows, rows)
        act_ref[c * rows:(c + 1) * rows, :] = (a * jax.nn.sigmoid(a) * b).astype(BF16)

    down = jnp.dot(act_ref[...], wd_ref[...], preferred_element_type=F32)

    @pl.when(f == 0)
    def _():
        acc_ref[...] = down

    @pl.when(f != 0)
    def _():
        acc_ref[...] += down

    @pl.when(f == n_f - 1)
    def _():
        def body(c, carry):
            r = pl.ds(pl.multiple_of(c * rows, rows), rows)
            y = x_ref[r, :] + acc_ref[r, :]
            o_ref[r, :] = _rms(y, fg_ref[...]) if final else y
            return carry
        lax.fori_loop(0, tm // rows, body, 0)


def _ffn(x1, ffn_norm, w_up, ffn_conv_w, w_down, final_norm, *, seq, final, tm=512, tf=512):
    T = x1.shape[0]
    h8 = SUBLANES_F32
    rb = tm // h8
    last_hb = T // h8 - 1
    n_f = D_FF // tf
    return pl.pallas_call(
        functools.partial(_ffn_kernel, tm=tm, tf=tf, tiles_per_seq=seq // tm, n_f=n_f, final=final),
        grid=(T // tm, n_f),
        in_specs=[
            pl.BlockSpec((tm, D_MODEL), lambda i, f: (i, 0)),
            pl.BlockSpec((h8, D_MODEL), lambda i, f: (jnp.maximum(i * rb - 1, 0), 0)),
            pl.BlockSpec((h8, D_MODEL), lambda i, f: (jnp.minimum((i + 1) * rb, last_hb), 0)),
            pl.BlockSpec((1, D_MODEL), lambda i, f: (0, 0)),
            pl.BlockSpec((D_MODEL, tf), lambda i, f: (0, f)),
            pl.BlockSpec((D_MODEL, tf), lambda i, f: (0, n_f + f)),
            pl.BlockSpec((3, tf), lambda i, f: (0, f)),
            pl.BlockSpec((3, tf), lambda i, f: (0, n_f + f)),
            pl.BlockSpec((tf, D_MODEL), lambda i, f: (f, 0)),
            pl.BlockSpec((1, D_MODEL), lambda i, f: (0, 0)),
        ],
        out_specs=pl.BlockSpec((tm, D_MODEL), lambda i, f: (i, 0)),
        out_shape=jax.ShapeDtypeStruct((T, D_MODEL), F32),
        scratch_shapes=[
            pltpu.VMEM((tm + 2 * h8, D_MODEL), BF16),
            pltpu.VMEM((tm + 2 * h8, tf), F32),
            pltpu.VMEM((tm + 2 * h8, tf), F32),
            pltpu.VMEM((tm, tf), BF16),
            pltpu.VMEM((tm, D_MODEL), F32),
        ],
        compiler_params=_params("parallel", "arbitrary"),
        name="ffn",
    )(x1, x1, x1, ffn_norm, w_up, w_up, ffn_conv_w, ffn_conv_w, w_down, final_norm)


def _pack_w_uq(w_uq):
    w = w_uq.reshape(Q_LORA, MLA_HEADS, QK_NOPE + QK_ROPE)
    nope = w[:, :, :QK_NOPE].reshape(Q_LORA, MLA_HEADS * QK_NOPE)
    rope = jnp.pad(w[:, :, QK_NOPE:], ((0, 0), (0, 0), (0, LANES - QK_ROPE))).reshape(Q_LORA, MLA_HEADS * LANES)
    return jnp.concatenate([nope, rope], axis=1).astype(BF16)


def kernel(x, mem, positions, mix_norm, w_in, q_norm, w_uq, kv_norm, w_ukv, w_o_mla, conv_w, w_out_conv,
           mem_norm, w_mem_kv, w_o_mem, gate_bias, w_o, ffn_norm, w_up, ffn_conv_w, w_down, final_norm):
    B, S, D = x.shape
    M = mem.shape[1]
    T = B * S
    depth = w_in.shape[0]
    x2 = x.reshape(T, D)
    mem2 = mem.reshape(B * M, D)
    pos = positions.reshape(T, 1)
    inv_freq = jnp.power(ROPE_THETA, -jnp.arange(0, QK_ROPE, 2, dtype=F32) / QK_ROPE)
    invf = jnp.concatenate([inv_freq, inv_freq, jnp.zeros((LANES - QK_ROPE,), F32)]).reshape(1, LANES)
    kr0 = Q_LORA + KV_LORA

    for l in range(depth):
        wi = w_in[l]
        w_main = jnp.concatenate([wi[:, :kr0], wi[:, kr0 + QK_ROPE:]], axis=1).astype(BF16)
        w_kr = jnp.pad(wi[:, kr0:kr0 + QK_ROPE], ((0, 0), (0, LANES - QK_ROPE))).astype(BF16)

        z, kr = _in_proj(x2, mix_norm[l].reshape(1, D), w_main, w_kr)
        q, k, v = _qkv(z, kr, pos, invf, q_norm[l].reshape(1, -1), kv_norm[l].reshape(1, -1),
                       _pack_w_uq(w_uq[l]), w_ukv[l].astype(BF16))
        o_mla = _mla(q, k, v, batch=B, seq=S)
        kvm = _mem_kv(mem2, mem_norm[l].reshape(1, D), w_mem_kv[l].astype(BF16))
        yx = _mem_attn(z, kvm, seq=S, mem_len=M)
        merged = _merge(o_mla, z, yx, gate_bias[l].reshape(1, -1), conv_w[l],
                        w_o_mla[l].astype(BF16), w_out_conv[l].astype(BF16), w_o_mem[l].astype(BF16), seq=S)
        x2 = _out_proj(merged, w_o[l].astype(BF16), x2)
        x2 = _ffn(x2, ffn_norm[l].reshape(1, D), w_up[l].astype(BF16), ffn_conv_w[l],
                  w_down[l].astype(BF16), final_norm.reshape(1, D), seq=S, final=(l == depth - 1))
    return x2.reshape(B, S, D)
```

```python
import functools

import jax
import jax.numpy as jnp
from jax import lax
from jax.experimental import pallas as pl
from jax.experimental.pallas import tpu as pltpu

F32 = jnp.float32
BF16 = jnp.bfloat16

D_MODEL = 2048
MLA_HEADS = 8
Q_LORA = 512
KV_LORA = 512
QK_NOPE = 128
QK_ROPE = 64
V_HEAD = 128
ROPE_THETA = 10000.0
CONV_WIDTH = 1024
X_HEADS = 4
X_HEAD_DIM = 256
X_WIDTH = X_HEADS * X_HEAD_DIM
D_FF = 5632
N_BRANCH = 3
EPS = 1e-6
LOG2_E = 1.4426950408889634

LANES = 128
SUBLANES_F32 = 8
SUBLANES_BF16 = 16
HEAD_PAD = 256
VMEM_LIMIT = 56 * 1024 * 1024

Z_G = 0
Z_CQ = Z_G + N_BRANCH * D_MODEL
Z_CKV = Z_CQ + Q_LORA
Z_CV = Z_CKV + KV_LORA
Z_CB = Z_CV + CONV_WIDTH
Z_CC = Z_CB + CONV_WIDTH
Z_QX = Z_CC + CONV_WIDTH
Z_COLS = Z_QX + X_WIDTH


def _params(*sem):
    return pltpu.CompilerParams(dimension_semantics=sem, vmem_limit_bytes=VMEM_LIMIT)


def _rms(x, g):
    inv = lax.rsqrt(jnp.mean(x * x, axis=-1, keepdims=True) + EPS)
    return x * inv * g


def _rms_rows_to(x_ref, g_ref, out_ref, rows, chunk, out_row0=0):
    def body(c, carry):
        r0 = pl.multiple_of(c * chunk, chunk)
        x = x_ref[pl.ds(r0, chunk), :].astype(F32)
        out_ref[pl.ds(out_row0 + r0, chunk), :] = _rms(x, g_ref[...]).astype(out_ref.dtype)
        return carry
    lax.fori_loop(0, rows // chunk, body, 0)


def _in_proj_kernel(x_ref, g_ref, w_ref, wkr_ref, z_ref, kr_ref, h_ref, *, tm):
    @pl.when(pl.program_id(1) == 0)
    def _():
        _rms_rows_to(x_ref, g_ref, h_ref, tm, 128)
        kr_ref[...] = jnp.dot(h_ref[...], wkr_ref[...], preferred_element_type=F32)

    z_ref[...] = jnp.dot(h_ref[...], w_ref[...], preferred_element_type=F32).astype(BF16)


def _in_proj(x2, g, w_main, w_kr, *, tm=1024, tn=1024):
    T = x2.shape[0]
    return pl.pallas_call(
        functools.partial(_in_proj_kernel, tm=tm),
        grid=(T // tm, Z_COLS // tn),
        in_specs=[
            pl.BlockSpec((tm, D_MODEL), lambda i, j: (i, 0)),
            pl.BlockSpec((1, D_MODEL), lambda i, j: (0, 0)),
            pl.BlockSpec((D_MODEL, tn), lambda i, j: (0, j)),
            pl.BlockSpec((D_MODEL, LANES), lambda i, j: (0, 0)),
        ],
        out_specs=[
            pl.BlockSpec((tm, tn), lambda i, j: (i, j)),
            pl.BlockSpec((tm, LANES), lambda i, j: (i, 0)),
        ],
        out_shape=[
            jax.ShapeDtypeStruct((T, Z_COLS), BF16),
            jax.ShapeDtypeStruct((T, LANES), F32),
        ],
        scratch_shapes=[pltpu.VMEM((tm, D_MODEL), BF16)],
        compiler_params=_params("parallel", "arbitrary"),
        name="in_proj",
    )(x2, g, w_main, w_kr)


def _qkv_kernel(cq_ref, ckv_ref, kr_ref, pos_ref, invf_ref, qn_ref, kvn_ref, wuq_ref, wukv_ref,
                q_ref, k_ref, v_ref, *, tm):
    ang = pos_ref[...].astype(F32) * invf_ref[...]
    cos = jnp.cos(ang)
    sin = jnp.sin(ang)
    lane = lax.broadcasted_iota(jnp.int32, (tm, LANES), 1)
    half = QK_ROPE // 2
    c_tab = jnp.where(lane < QK_ROPE, cos, 0.0)
    s_lo = jnp.where(lane < half, -sin, 0.0)
    s_hi = jnp.where((lane >= half) & (lane < QK_ROPE), sin, 0.0)

    def rope(t):
        return t * c_tab + pltpu.roll(t, LANES - half, 1) * s_lo + pltpu.roll(t, half, 1) * s_hi

    scale = (QK_NOPE + QK_ROPE) ** -0.5 * LOG2_E
    cqn =_rms(cq_ref[...].astype(F32), qn_ref[...]).astype(BF16)
    q = jnp.dot(cqn, wuq_ref[...], preferred_element_type=F32)
    nope_cols = MLA_HEADS * QK_NOPE
    for h in range(MLA_HEADS):
        q_ref[:, h * HEAD_PAD:h * HEAD_PAD + QK_NOPE] = (
            q[:, h * QK_NOPE:(h + 1) * QK_NOPE] * scale).astype(BF16)
        q_ref[:, h * HEAD_PAD + QK_NOPE:(h + 1) * HEAD_PAD] = (
            rope(q[:, nope_cols + h * LANES:nope_cols + (h + 1) * LANES]) * scale).astype(BF16)

    ckvn = _rms(ckv_ref[...].astype(F32), kvn_ref[...]).astype(BF16)
    kv = jnp.dot(ckvn, wukv_ref[...], preferred_element_type=F32)
    k_rope = rope(kr_ref[...]).astype(BF16)
    ones = jnp.ones((tm, V_HEAD), BF16)
    for h in range(MLA_HEADS):
        k_ref[:, h * HEAD_PAD:h * HEAD_PAD + QK_NOPE] = kv[:, h * HEAD_PAD:h * HEAD_PAD + QK_NOPE].astype(BF16)
        k_ref[:, h * HEAD_PAD + QK_NOPE:(h + 1) * HEAD_PAD] = k_rope
        v_ref[:, 2 * h * V_HEAD:(2 * h + 1) * V_HEAD] = kv[:, h * HEAD_PAD + QK_NOPE:(h + 1) * HEAD_PAD].astype(BF16)
        v_ref[:, (2 * h + 1) * V_HEAD:(2 * h + 2) * V_HEAD] = ones


def _qkv(z, kr, pos, invf, q_norm, kv_norm, w_uq_p, w_ukv, *, tm=512):
    T = z.shape[0]
    qk_cols = MLA_HEADS * HEAD_PAD
    return pl.pallas_call(
        functools.partial(_qkv_kernel, tm=tm),
        grid=(T // tm,),
        in_specs=[
            pl.BlockSpec((tm, Q_LORA), lambda i: (i, Z_CQ // Q_LORA)),
            pl.BlockSpec((tm, KV_LORA), lambda i: (i, Z_CKV // KV_LORA)),
            pl.BlockSpec((tm, LANES), lambda i: (i, 0)),
            pl.BlockSpec((tm, 1), lambda i: (i, 0)),
            pl.BlockSpec((1, LANES), lambda i: (0, 0)),
            pl.BlockSpec((1, Q_LORA), lambda i: (0, 0)),
            pl.BlockSpec((1, KV_LORA), lambda i: (0, 0)),
            pl.BlockSpec((Q_LORA, qk_cols), lambda i: (0, 0)),
            pl.BlockSpec((KV_LORA, qk_cols), lambda i: (0, 0)),
        ],
        out_specs=[
            pl.BlockSpec((tm, qk_cols), lambda i: (i, 0)),
            pl.BlockSpec((tm, qk_cols), lambda i: (i, 0)),
            pl.BlockSpec((tm, 2 * MLA_HEADS * V_HEAD), lambda i: (i, 0)),
        ],
        out_shape=[
            jax.ShapeDtypeStruct((T, qk_cols), BF16),
            jax.ShapeDtypeStruct((T, qk_cols), BF16),
            jax.ShapeDtypeStruct((T, 2 * MLA_HEADS * V_HEAD), BF16),
        ],
        compiler_params=_params("parallel"),
        name="qkv",
    )(z, z, kr, pos, invf, q_norm, kv_norm, w_uq_p, w_ukv)


def _mla_kernel(q_ref, k_ref, v_ref, o_ref, *, seq, tq):
    for c in range(seq // tq):
        r = slice(c * tq, (c + 1) * tq)
        s = lax.dot_general(q_ref[r, :], k_ref[...], (((1,), (1,)), ((), ())),
                            preferred_element_type=F32)
        m = jnp.max(s, axis=-1, keepdims=True)
        p = jnp.exp2(s - m).astype(BF16)
        ov = jnp.dot(p, v_ref[...], preferred_element_type=F32)
        o_ref[r, :] = (ov[:, :V_HEAD] / ov[:, V_HEAD:]).astype(BF16)


def _mla(q, k, v, *, batch, seq, tq=512):
    T = q.shape[0]
    return pl.pallas_call(
        functools.partial(_mla_kernel, seq=seq, tq=tq),
        grid=(batch, MLA_HEADS),
        in_specs=[
            pl.BlockSpec((seq, HEAD_PAD), lambda b, h: (b, h)),
            pl.BlockSpec((seq, HEAD_PAD), lambda b, h: (b, h)),
            pl.BlockSpec((seq, 2 * V_HEAD), lambda b, h: (b, h)),
        ],
        out_specs=pl.BlockSpec((seq, V_HEAD), lambda b, h: (b, h)),
        out_shape=jax.ShapeDtypeStruct((T, MLA_HEADS * V_HEAD), BF16),
        compiler_params=_params("parallel", "parallel"),
        name="mla_attn",
    )(q, k, v)


def _mem_kv_kernel(m_ref, g_ref, w_ref, o_ref, h_ref, *, tm):
    @pl.when(pl.program_id(1) == 0)
    def _():
        _rms_rows_to(m_ref, g_ref, h_ref, tm, 128)

    o_ref[...] = jnp.dot(h_ref[...], w_ref[...], preferred_element_type=F32).astype(BF16)


def _mem_kv(mem2, g, w, *, tm=1024, tn=1024):
    R = mem2.shape[0]
    N = w.shape[1]
    return pl.pallas_call(
        functools.partial(_mem_kv_kernel, tm=tm),
        grid=(R // tm, N // tn),
        in_specs=[
            pl.BlockSpec((tm, D_MODEL), lambda i, j: (i, 0)),
            pl.BlockSpec((1, D_MODEL), lambda i, j: (0, 0)),
            pl.BlockSpec((D_MODEL, tn), lambda i, j: (0, j)),
        ],
        out_specs=pl.BlockSpec((tm, tn), lambda i, j: (i, j)),
        out_shape=jax.ShapeDtypeStruct((R, N), BF16),
        scratch_shapes=[pltpu.VMEM((tm, D_MODEL), BF16)],
        compiler_params=_params("parallel", "arbitrary"),
        name="mem_kv",
    )(mem2, g, w)


def _mem_attn_kernel(q_ref, kv_ref, o_ref):
    scale = X_HEAD_DIM ** -0.5
    for h in range(X_HEADS):
        c = slice(h * X_HEAD_DIM, (h + 1) * X_HEAD_DIM)
        k = kv_ref[:, h * X_HEAD_DIM:(h + 1) * X_HEAD_DIM]
        v = kv_ref[:, X_WIDTH + h * X_HEAD_DIM:X_WIDTH + (h + 1) * X_HEAD_DIM]
        s = lax.dot_general(q_ref[:, c], k, (((1,), (1,)), ((), ())), preferred_element_type=F32) * scale
        m = jnp.max(s, axis=-1, keepdims=True)
        p = jnp.exp(s - m)
        l = jnp.sum(p, axis=-1, keepdims=True)
        o = jnp.dot(p.astype(BF16), v, preferred_element_type=F32)
        o_ref[:, c] = (o / l).astype(BF16)


def _mem_attn(z, kvm, *, seq, mem_len, tm=1024):
    T = z.shape[0]
    return pl.pallas_call(
        _mem_attn_kernel,
        grid=(T // tm,),
        in_specs=[
            pl.BlockSpec((tm, X_WIDTH), lambda i: (i, Z_QX // X_WIDTH)),
            pl.BlockSpec((mem_len, 2 * X_WIDTH), lambda i: (i // (seq // tm), 0)),
        ],
        out_specs=pl.BlockSpec((tm, X_WIDTH), lambda i: (i, 0)),
        out_shape=jax.ShapeDtypeStruct((T, X_WIDTH), BF16),
        compiler_params=_params("parallel"),
        name="mem_attn",
    )(z, kvm)


def _merge_kernel(o_ref, cv_ref, cb_ref, cc_ref, cvp_ref, ccp_ref, cvn_ref, ccn_ref, yx_ref,
                  g0_ref, g1_ref, g2_ref, b_ref, cw_ref,
                  wa_ref, wc_ref, wm_ref, out_ref, pext_ref, *, tm, tn, tiles_per_seq):
    pos_in_seq = pl.program_id(0) % tiles_per_seq
    keep_prev = jnp.where(pos_in_seq == 0, 0.0, 1.0)
    keep_next = jnp.where(pos_in_seq == tiles_per_seq - 1, 0.0, 1.0)
    h8 = SUBLANES_F32
    prev = (cvp_ref[h8:, :].astype(F32) * ccp_ref[h8:, :].astype(F32)) * keep_prev
    nxt = (cvn_ref[:h8, :].astype(F32) * ccn_ref[:h8, :].astype(F32)) * keep_next
    pext_ref[0:h8, :] = prev
    pext_ref[tm + h8:tm + 2 * h8, :] = nxt
    pext_ref[h8:tm + h8, :] = cv_ref[...].astype(F32) * cc_ref[...].astype(F32)
    conv = (cw_ref[0:1, :] * pext_ref[h8 - 1:tm + h8 - 1, :] + cw_ref[1:2, :] * pext_ref[h8:tm + h8, :]
            + cw_ref[2:3, :] * pext_ref[h8 + 1:tm + h8 + 1, :])
    cbr = (cb_ref[...].astype(F32) * conv).astype(BF16)

    def gate(g_ref, k, c):
        return jax.nn.sigmoid(g_ref[:, c].astype(F32) + b_ref[:, k * D_MODEL + c.start:k * D_MODEL + c.stop])

    for n in range(D_MODEL // tn):
        c = slice(n * tn, (n + 1) * tn)
        y = gate(g0_ref, 0, c) * jnp.dot(o_ref[...], wa_ref[:, c], preferred_element_type=F32)
        y += gate(g1_ref, 1, c) * jnp.dot(cbr, wc_ref[:, c], preferred_element_type=F32)
        y += gate(g2_ref, 2, c) * jnp.dot(yx_ref[...], wm_ref[:, c], preferred_element_type=F32)
        out_ref[:, c] = y.astype(BF16)


def _merge(o_mla, z, yx, gate_bias, conv_w, w_o_mla, w_out_conv, w_o_mem, *, seq, tm=512, tn=1024):
    T = z.shape[0]
    hb = SUBLANES_BF16
    rb = tm // hb
    last_hb = T // hb - 1
    cw = CONV_WIDTH
    once = pl.Buffered(1)

    def zcol(c0, width):
        return c0 // width

    def gate_spec(k):
        return pl.BlockSpec((tm, D_MODEL), lambda i, k=k: (i, Z_G // D_MODEL + k))

    def prev_spec(c0):
        return pl.BlockSpec((hb, cw), lambda i: (jnp.maximum(i * rb - 1, 0), zcol(c0, cw)))

    def next_spec(c0):
        return pl.BlockSpec((hb, cw), lambda i: (jnp.minimum((i + 1) * rb, last_hb), zcol(c0, cw)))

    def w_spec():
        return pl.BlockSpec((cw, D_MODEL), lambda i: (0, 0), pipeline_mode=once)

    assert Z_G % D_MODEL == 0
    return pl.pallas_call(
        functools.partial(_merge_kernel, tm=tm, tn=tn, tiles_per_seq=seq // tm),
        grid=(T // tm,),
        in_specs=[
            pl.BlockSpec((tm, MLA_HEADS * V_HEAD), lambda i: (i, 0)),
            pl.BlockSpec((tm, cw), lambda i: (i, zcol(Z_CV, cw))),
            pl.BlockSpec((tm, cw), lambda i: (i, zcol(Z_CB, cw))),
            pl.BlockSpec((tm, cw), lambda i: (i, zcol(Z_CC, cw))),
            prev_spec(Z_CV), prev_spec(Z_CC), next_spec(Z_CV), next_spec(Z_CC),
            pl.BlockSpec((tm, X_WIDTH), lambda i: (i, 0)),
            gate_spec(0), gate_spec(1), gate_spec(2),
            pl.BlockSpec((1, N_BRANCH * D_MODEL), lambda i: (0, 0), pipeline_mode=once),
            pl.BlockSpec((3, cw), lambda i: (0, 0), pipeline_mode=once),
            w_spec(), w_spec(), w_spec(),
        ],
        out_specs=pl.BlockSpec((tm, D_MODEL), lambda i: (i, 0)),
        out_shape=jax.ShapeDtypeStruct((T, D_MODEL), BF16),
        scratch_shapes=[pltpu.VMEM((tm + 2 * SUBLANES_F32, cw), F32)],
        compiler_params=_params("parallel"),
        name="merge",
    )(o_mla, z, z, z, z, z, z, z, yx, z, z, z, gate_bias, conv_w, w_o_mla, w_out_conv, w_o_mem)


def _out_proj_kernel(m_ref, w_ref, x_ref, o_ref):
    o_ref[...] = x_ref[...] + jnp.dot(m_ref[...], w_ref[...], preferred_element_type=F32)


def _out_proj(merged, w_o, x2, *, tm=1024, tn=1024):
    T = x2.shape[0]
    return pl.pallas_call(
        _out_proj_kernel,
        grid=(T // tm, D_MODEL // tn),
        in_specs=[
            pl.BlockSpec((tm, D_MODEL), lambda i, j: (i, 0)),
            pl.BlockSpec((D_MODEL, tn), lambda i, j: (0, j)),
            pl.BlockSpec((tm, tn), lambda i, j: (i, j)),
        ],
        out_specs=pl.BlockSpec((tm, tn), lambda i, j: (i, j)),
        out_shape=jax.ShapeDtypeStruct((T, D_MODEL), F32),
        compiler_params=_params("parallel", "arbitrary"),
        name="out_proj",
    )(merged, w_o, x2)


FFN_COL_CHUNK = 256
FFN_TF = 512


def _ffn_kernel(x_ref, xp_ref, xn_ref, g_ref, wu_ref, ca_ref, cb_ref, wd_ref, fg_ref,
                o_ref, hn_ref, ua_ref, ub_ref, acc_ref, *, tm, tf, tiles_per_seq, n_f, final):
    i = pl.program_id(0)
    f = pl.program_id(1)
    h8 = SUBLANES_F32
    cw = FFN_COL_CHUNK
    rows = 128

    @pl.when(f == 0)
    def _():
        pos_in_seq = i % tiles_per_seq
        keep_prev = jnp.where(pos_in_seq == 0, 0.0, 1.0)
        keep_next = jnp.where(pos_in_seq == tiles_per_seq - 1, 0.0, 1.0)
        halo = jnp.concatenate([_rms(xp_ref[...], g_ref[...]) * keep_prev,
                                _rms(xn_ref[...], g_ref[...]) * keep_next], axis=0)
        hn_ref[tm:tm + 2 * h8, :] = halo.astype(BF16)
        _rms_rows_to(x_ref, g_ref, hn_ref, tm, rows)
        acc_ref[...] = jnp.zeros_like(acc_ref)

    def up(c0, u_ref):
        r = jnp.dot(hn_ref[...], wu_ref[0, :, c0:c0 + cw], preferred_element_type=F32)
        u_ref[h8:tm + h8, :] = r[:tm]
        u_ref[0:h8, :] = r[tm:tm + h8]
        u_ref[tm + h8:tm + 2 * h8, :] = r[tm + h8:]

    def conv(u_ref, c_ref, c0):
        return (c_ref[0:1, c0:c0 + cw] * u_ref[h8 - 1:tm + h8 - 1, :]
                + c_ref[1:2, c0:c0 + cw] * u_ref[h8:tm + h8, :]
                + c_ref[2:3, c0:c0 + cw] * u_ref[h8 + 1:tm + h8 + 1, :])

    down = None
    for n in range(tf // cw):
        c0 = n * cw
        up(c0, ua_ref.at[n])
        up(tf + c0, ub_ref.at[n])
        a = conv(ua_ref.at[n], ca_ref, c0)
        b = conv(ub_ref.at[n], cb_ref, c0)
        act = (a * jax.nn.sigmoid(a) * b).astype(BF16)
        d = jnp.dot(act, wd_ref[c0:c0 + cw, :], preferred_element_type=F32)
        down = d if down is None else down + d
    acc_ref[...] += down

    @pl.when(f == n_f - 1)
    def _():
        def body(c, carry):
            r = pl.ds(pl.multiple_of(c * rows, rows), rows)
            y = x_ref[r, :] + acc_ref[r, :]
            o_ref[r, :] = _rms(y, fg_ref[...]) if final else y
            return carry
        lax.fori_loop(0, tm // rows, body, 0)


def _ffn(x1, ffn_norm, w_up_blk, ffn_conv_w, w_down, final_norm, *, seq, final, tm=512):
    T = x1.shape[0]
    h8 = SUBLANES_F32
    rb = tm // h8
    last_hb = T // h8 - 1
    n_f = w_up_blk.shape[0]
    tf = w_up_blk.shape[2] // 2
    n_chunks = tf // FFN_COL_CHUNK
    return pl.pallas_call(
        functools.partial(_ffn_kernel, tm=tm, tf=tf, tiles_per_seq=seq // tm, n_f=n_f, final=final),
        grid=(T // tm, n_f),
        in_specs=[
            pl.BlockSpec((tm, D_MODEL), lambda i, f: (i, 0)),
            pl.BlockSpec((h8, D_MODEL), lambda i, f: (jnp.maximum(i * rb - 1, 0), 0)),
            pl.BlockSpec((h8, D_MODEL), lambda i, f: (jnp.minimum((i + 1) * rb, last_hb), 0)),
            pl.BlockSpec((1, D_MODEL), lambda i, f: (0, 0)),
            pl.BlockSpec((1, D_MODEL, 2 * tf), lambda i, f: (f, 0, 0)),
            pl.BlockSpec((3, tf), lambda i, f: (0, f)),
            pl.BlockSpec((3, tf), lambda i, f: (0, n_f + f)),
            pl.BlockSpec((tf, D_MODEL), lambda i, f: (f, 0)),
            pl.BlockSpec((1, D_MODEL), lambda i, f: (0, 0)),
        ],
        out_specs=pl.BlockSpec((tm, D_MODEL), lambda i, f: (i, 0)),
        out_shape=jax.ShapeDtypeStruct((T, D_MODEL), F32),
        scratch_shapes=[
            pltpu.VMEM((tm + 2 * h8, D_MODEL), BF16),
            pltpu.VMEM((n_chunks, tm + 2 * h8, FFN_COL_CHUNK), F32),
            pltpu.VMEM((n_chunks, tm + 2 * h8, FFN_COL_CHUNK), F32),
            pltpu.VMEM((tm, D_MODEL), F32),
        ],
        compiler_params=_params("parallel", "arbitrary"),
        name="ffn",
    )(x1, x1, x1, ffn_norm, w_up_blk, ffn_conv_w, ffn_conv_w, w_down, final_norm)


def _block_w_up(w_up, tf):
    d = w_up.shape[0]
    w = w_up.reshape(d, 2, D_FF // tf, tf)
    return jnp.transpose(w, (2, 0, 1, 3)).reshape(D_FF // tf, d, 2 * tf).astype(BF16)


def _pack_w_uq(w_uq):
    w = w_uq.reshape(Q_LORA, MLA_HEADS, QK_NOPE + QK_ROPE)
    nope = w[:, :, :QK_NOPE].reshape(Q_LORA, MLA_HEADS * QK_NOPE)
    rope = jnp.pad(w[:, :, QK_NOPE:], ((0, 0), (0, 0), (0, LANES - QK_ROPE))).reshape(Q_LORA, MLA_HEADS * LANES)
    return jnp.concatenate([nope, rope], axis=1).astype(BF16)


def kernel(x, mem, positions, mix_norm, w_in, q_norm, w_uq, kv_norm, w_ukv, w_o_mla, conv_w, w_out_conv,
           mem_norm, w_mem_kv, w_o_mem, gate_bias, w_o, ffn_norm, w_up, ffn_conv_w, w_down, final_norm):
    B, S, D = x.shape
    M = mem.shape[1]
    T = B * S
    depth = w_in.shape[0]
    x2 = x.reshape(T, D)
    mem2 = mem.reshape(B * M, D)
    pos = positions.reshape(T, 1)
    inv_freq = jnp.power(ROPE_THETA, -jnp.arange(0, QK_ROPE, 2, dtype=F32) / QK_ROPE)
    invf = jnp.concatenate([inv_freq, inv_freq, jnp.zeros((LANES - QK_ROPE,), F32)]).reshape(1, LANES)
    kr0 = Q_LORA + KV_LORA

    for l in range(depth):
        wi = w_in[l]
        g0 = wi.shape[1] - N_BRANCH * D_MODEL
        w_main = jnp.concatenate([wi[:, g0:].astype(BF16), wi[:, :kr0].astype(BF16),
                                  wi[:, kr0 + QK_ROPE:g0].astype(BF16)], axis=1)
        w_kr = jnp.pad(wi[:, kr0:kr0 + QK_ROPE], ((0, 0), (0, LANES - QK_ROPE))).astype(BF16)

        z, kr = _in_proj(x2, mix_norm[l].reshape(1, D), w_main, w_kr)
        q, k, v = _qkv(z, kr, pos, invf, q_norm[l].reshape(1, -1), kv_norm[l].reshape(1, -1),
                       _pack_w_uq(w_uq[l]), w_ukv[l].astype(BF16))
        o_mla = _mla(q, k, v, batch=B, seq=S)
        kvm = _mem_kv(mem2, mem_norm[l].reshape(1, D), w_mem_kv[l].astype(BF16))
        yx = _mem_attn(z, kvm, seq=S, mem_len=M)
        merged = _merge(o_mla, z, yx, gate_bias[l].reshape(1, -1), conv_w[l],
                        w_o_mla[l].astype(BF16), w_out_conv[l].astype(BF16), w_o_mem[l].astype(BF16), seq=S)
        x2 = _out_proj(merged, w_o[l].astype(BF16), x2)
        x2 = _ffn(x2, ffn_norm[l].reshape(1, D), _block_w_up(w_up[l], FFN_TF), ffn_conv_w[l],
                  w_down[l].astype(BF16), final_norm.reshape(1, D), seq=S, final=(l == depth - 1))
    return x2.reshape(B, S, D)
```

```python
import functools

import jax
import jax.numpy as jnp
from jax import lax
from jax.experimental import pallas as pl
from jax.experimental.pallas import tpu as pltpu

F32 = jnp.float32
BF16 = jnp.bfloat16

D_MODEL = 2048
MLA_HEADS = 8
Q_LORA = 512
KV_LORA = 512
QK_NOPE = 128
QK_ROPE = 64
V_HEAD = 128
ROPE_THETA = 10000.0
CONV_WIDTH = 1024
X_HEADS = 4
X_HEAD_DIM = 256
X_WIDTH = X_HEADS * X_HEAD_DIM
D_FF = 5632
N_BRANCH = 3
EPS = 1e-6
LOG2_E = 1.4426950408889634

LANES = 128
SUBLANES_F32 = 8
SUBLANES_BF16 = 16
HEAD_PAD = 256
VMEM_LIMIT = 56 * 1024 * 1024

Z_G = 0
Z_CQ = Z_G + N_BRANCH * D_MODEL
Z_CKV = Z_CQ + Q_LORA
Z_CV = Z_CKV + KV_LORA
Z_CB = Z_CV + CONV_WIDTH
Z_CC = Z_CB + CONV_WIDTH
Z_QX = Z_CC + CONV_WIDTH
Z_COLS = Z_QX + X_WIDTH


_NT = (((1,), (1,)), ((), ()))


def _params(*sem):
    return pltpu.CompilerParams(dimension_semantics=sem, vmem_limit_bytes=VMEM_LIMIT)


def _rms(x, g):
    inv = lax.rsqrt(jnp.mean(x * x, axis=-1, keepdims=True) + EPS)
    return x * inv * g


def _rms_rows_to(x_ref, g_ref, out_ref, rows, chunk, out_row0=0):
    def body(c, carry):
        r0 = pl.multiple_of(c * chunk, chunk)
        x = x_ref[pl.ds(r0, chunk), :].astype(F32)
        out_ref[pl.ds(out_row0 + r0, chunk), :] = _rms(x, g_ref[...]).astype(out_ref.dtype)
        return carry
    lax.fori_loop(0, rows // chunk, body, 0)


IN_TN = 1024
FFN_TF = 512


N_GATE_BLK = N_BRANCH * D_MODEL // IN_TN


def _pack_w_in_kernel(a_ref, b_ref, kr_ref, wblk_ref, wkr_ref):
    j = pl.program_id(0)

    @pl.when(j == N_GATE_BLK)
    def _():
        wblk_ref[0] = a_ref[...].astype(BF16)

    @pl.when(j != N_GATE_BLK)
    def _():
        wblk_ref[0, :IN_TN - QK_ROPE, :] = a_ref[QK_ROPE:, :].astype(BF16)
        wblk_ref[0, IN_TN - QK_ROPE:, :] = b_ref[...].astype(BF16)

    @pl.when(j == 0)
    def _():
        wkr_ref[:QK_ROPE, :] = kr_ref[...].astype(BF16)
        wkr_ref[QK_ROPE:, :] = jnp.zeros((LANES - QK_ROPE, wkr_ref.shape[1]), BF16)


def _pack_w_in(wi_t):
    cols, d = wi_t.shape
    kr0 = Q_LORA + KV_LORA
    assert cols == Z_COLS + QK_ROPE and kr0 == IN_TN and IN_TN % QK_ROPE == 0
    n_blk = Z_COLS // IN_TN
    n_tail = n_blk - 1 - N_GATE_BLK
    sub = IN_TN // QK_ROPE

    def src_blk(j):
        return jnp.where(j < N_GATE_BLK, j + 1 + n_tail, jnp.where(j == N_GATE_BLK, 0, j - N_GATE_BLK))

    return pl.pallas_call(
        _pack_w_in_kernel,
        grid=(n_blk,),
        in_specs=[
            pl.BlockSpec((IN_TN, d), lambda j: (src_blk(j), 0)),
            pl.BlockSpec((QK_ROPE, d), lambda j: ((src_blk(j) + 1) * sub, 0)),
            pl.BlockSpec((QK_ROPE, d), lambda j: (kr0 // QK_ROPE, 0)),
        ],
        out_specs=[
            pl.BlockSpec((1, IN_TN, d), lambda j: (j, 0, 0)),
            pl.BlockSpec((LANES, d), lambda j: (0, 0)),
        ],
        out_shape=[
            jax.ShapeDtypeStruct((n_blk, IN_TN, d), BF16),
            jax.ShapeDtypeStruct((LANES, d), BF16),
        ],
        compiler_params=_params("arbitrary"),
        name="pack_w_in",
    )(wi_t, wi_t, wi_t)


def _pack_w_up_kernel(w_ref, o_ref):
    o_ref[0] = w_ref[...].astype(BF16)


def _pack_w_up(w_up):
    d = w_up.shape[0]
    n_f = D_FF // FFN_TF
    return pl.pallas_call(
        _pack_w_up_kernel,
        grid=(n_f, 2),
        in_specs=[pl.BlockSpec((d, FFN_TF), lambda f, h: (0, h * n_f + f))],
        out_specs=pl.BlockSpec((1, d, FFN_TF), lambda f, h: (f, 0, h)),
        out_shape=jax.ShapeDtypeStruct((n_f, d, 2 * FFN_TF), BF16),
        compiler_params=_params("parallel", "parallel"),
        name="pack_w_up",
    )(w_up)


def _in_proj_kernel(x_ref, g_ref, w_ref, wkr_ref, z_ref, kr_ref, h_ref, *, tm):
    @pl.when(pl.program_id(1) == 0)
    def _():
        _rms_rows_to(x_ref, g_ref, h_ref, tm, 128)
        kr_ref[...] = lax.dot_general(h_ref[...], wkr_ref[...], _NT, preferred_element_type=F32)

    z_ref[...] = lax.dot_general(h_ref[...], w_ref[0], _NT, preferred_element_type=F32).astype(BF16)


def _in_proj(x2, g, w_blk, w_kr, *, tm=1024):
    T = x2.shape[0]
    tn = IN_TN
    return pl.pallas_call(
        functools.partial(_in_proj_kernel, tm=tm),
        grid=(T // tm, Z_COLS // tn),
        in_specs=[
            pl.BlockSpec((tm, D_MODEL), lambda i, j: (i, 0)),
            pl.BlockSpec((1, D_MODEL), lambda i, j: (0, 0)),
            pl.BlockSpec((1, tn, D_MODEL), lambda i, j: (j, 0, 0)),
            pl.BlockSpec((LANES, D_MODEL), lambda i, j: (0, 0)),
        ],
        out_specs=[
            pl.BlockSpec((tm, tn), lambda i, j: (i, j)),
            pl.BlockSpec((tm, LANES), lambda i, j: (i, 0)),
        ],
        out_shape=[
            jax.ShapeDtypeStruct((T, Z_COLS), BF16),
            jax.ShapeDtypeStruct((T, LANES), F32),
        ],
        scratch_shapes=[pltpu.VMEM((tm, D_MODEL), BF16)],
        compiler_params=_params("parallel", "arbitrary"),
        name="in_proj",
    )(x2, g, w_blk, w_kr)


def _qkv_kernel(cq_ref, ckv_ref, kr_ref, pos_ref, invf_ref, qn_ref, kvn_ref, wuq_ref, wukv_ref,
                q_ref, k_ref, v_ref, *, tm):
    ang = pos_ref[...].astype(F32) * invf_ref[...]
    cos = jnp.cos(ang)
    sin = jnp.sin(ang)
    lane = lax.broadcasted_iota(jnp.int32, (tm, LANES), 1)
    half = QK_ROPE // 2
    c_tab = jnp.where(lane < QK_ROPE, cos, 0.0)
    s_lo = jnp.where(lane < half, -sin, 0.0)
    s_hi = jnp.where((lane >= half) & (lane < QK_ROPE), sin, 0.0)

    def rope(t):
        return t * c_tab + pltpu.roll(t, LANES - half, 1) * s_lo + pltpu.roll(t, half, 1) * s_hi

    scale = (QK_NOPE + QK_ROPE) ** -0.5 * LOG2_E
    cqn = _rms(cq_ref[...].astype(F32), qn_ref[...]).astype(BF16)
    q = jnp.dot(cqn, wuq_ref[...], preferred_element_type=F32)
    nope_cols = MLA_HEADS * QK_NOPE
    for h in range(MLA_HEADS):
        q_ref[:, h * HEAD_PAD:h * HEAD_PAD + QK_NOPE] = (
            q[:, h * QK_NOPE:(h + 1) * QK_NOPE] * scale).astype(BF16)
        q_ref[:, h * HEAD_PAD + QK_NOPE:(h + 1) * HEAD_PAD] = (
            rope(q[:, nope_cols + h * LANES:nope_cols + (h + 1) * LANES]) * scale).astype(BF16)

    ckvn = _rms(ckv_ref[...].astype(F32), kvn_ref[...]).astype(BF16)
    kv = jnp.dot(ckvn, wukv_ref[...], preferred_element_type=F32)
    k_rope = rope(kr_ref[...]).astype(BF16)
    ones = jnp.ones((tm, V_HEAD), BF16)
    for h in range(MLA_HEADS):
        k_ref[:, h * HEAD_PAD:h * HEAD_PAD + QK_NOPE] = kv[:, h * HEAD_PAD:h * HEAD_PAD + QK_NOPE].astype(BF16)
        k_ref[:, h * HEAD_PAD + QK_NOPE:(h + 1) * HEAD_PAD] = k_rope
        v_ref[:, 2 * h * V_HEAD:(2 * h + 1) * V_HEAD] = kv[:, h * HEAD_PAD + QK_NOPE:(h + 1) * HEAD_PAD].astype(BF16)
        v_ref[:, (2 * h + 1) * V_HEAD:(2 * h + 2) * V_HEAD] = ones


def _qkv(z, kr, pos, invf, q_norm, kv_norm, w_uq_p, w_ukv, *, tm=512):
    T = z.shape[0]
    qk_cols = MLA_HEADS * HEAD_PAD
    return pl.pallas_call(
        functools.partial(_qkv_kernel, tm=tm),
        grid=(T // tm,),
        in_specs=[
            pl.BlockSpec((tm, Q_LORA), lambda i: (i, Z_CQ // Q_LORA)),
            pl.BlockSpec((tm, KV_LORA), lambda i: (i, Z_CKV // KV_LORA)),
            pl.BlockSpec((tm, LANES), lambda i: (i, 0)),
            pl.BlockSpec((tm, 1), lambda i: (i, 0)),
            pl.BlockSpec((1, LANES), lambda i: (0, 0)),
            pl.BlockSpec((1, Q_LORA), lambda i: (0, 0)),
            pl.BlockSpec((1, KV_LORA), lambda i: (0, 0)),
            pl.BlockSpec((Q_LORA, qk_cols), lambda i: (0, 0)),
            pl.BlockSpec((KV_LORA, qk_cols), lambda i: (0, 0)),
        ],
        out_specs=[
            pl.BlockSpec((tm, qk_cols), lambda i: (i, 0)),
            pl.BlockSpec((tm, qk_cols), lambda i: (i, 0)),
            pl.BlockSpec((tm, 2 * MLA_HEADS * V_HEAD), lambda i: (i, 0)),
        ],
        out_shape=[
            jax.ShapeDtypeStruct((T, qk_cols), BF16),
            jax.ShapeDtypeStruct((T, qk_cols), BF16),
            jax.ShapeDtypeStruct((T, 2 * MLA_HEADS * V_HEAD), BF16),
        ],
        compiler_params=_params("parallel"),
        name="qkv",
    )(z, z, kr, pos, invf, q_norm, kv_norm, w_uq_p, w_ukv)


def _mla_kernel(q_ref, k_ref, v_ref, o_ref, *, seq, tq, heads):
    for h in range(heads):
        k = k_ref[:, h * HEAD_PAD:(h + 1) * HEAD_PAD]
        v = v_ref[:, 2 * h * V_HEAD:2 * (h + 1) * V_HEAD]
        for c in range(seq // tq):
            r = slice(c * tq, (c + 1) * tq)
            s = lax.dot_general(q_ref[r, h * HEAD_PAD:(h + 1) * HEAD_PAD], k, (((1,), (1,)), ((), ())),
                                preferred_element_type=F32)
            m = jnp.max(s, axis=-1, keepdims=True)
            p = jnp.exp2(s - m).astype(BF16)
            ov = jnp.dot(p, v, preferred_element_type=F32)
            o_ref[r, h * V_HEAD:(h + 1) * V_HEAD] = (ov[:, :V_HEAD] / ov[:, V_HEAD:]).astype(BF16)


def _mla(q, k, v, *, batch, seq, tq=512, heads=4):
    T = q.shape[0]
    return pl.pallas_call(
        functools.partial(_mla_kernel, seq=seq, tq=tq, heads=heads),
        grid=(batch, MLA_HEADS // heads),
        in_specs=[
            pl.BlockSpec((seq, heads * HEAD_PAD), lambda b, h: (b, h)),
            pl.BlockSpec((seq, heads * HEAD_PAD), lambda b, h: (b, h)),
            pl.BlockSpec((seq, heads * 2 * V_HEAD), lambda b, h: (b, h)),
        ],
        out_specs=pl.BlockSpec((seq, heads * V_HEAD), lambda b, h: (b, h)),
        out_shape=jax.ShapeDtypeStruct((T, MLA_HEADS * V_HEAD), BF16),
        compiler_params=_params("parallel", "parallel"),
        name="mla_attn",
    )(q, k, v)


def _mem_kv_kernel(m_ref, g_ref, w_ref, o_ref, h_ref, *, tm):
    @pl.when(pl.program_id(1) == 0)
    def _():
        _rms_rows_to(m_ref, g_ref, h_ref, tm, 128)

    o_ref[...] = jnp.dot(h_ref[...], w_ref[...], preferred_element_type=F32).astype(BF16)


def _mem_kv(mem2, g, w, *, tm=1024, tn=1024):
    R = mem2.shape[0]
    N = w.shape[1]
    return pl.pallas_call(
        functools.partial(_mem_kv_kernel, tm=tm),
        grid=(R // tm, N // tn),
        in_specs=[
            pl.BlockSpec((tm, D_MODEL), lambda i, j: (i, 0)),
            pl.BlockSpec((1, D_MODEL), lambda i, j: (0, 0)),
            pl.BlockSpec((D_MODEL, tn), lambda i, j: (0, j)),
        ],
        out_specs=pl.BlockSpec((tm, tn), lambda i, j: (i, j)),
        out_shape=jax.ShapeDtypeStruct((R, N), BF16),
        scratch_shapes=[pltpu.VMEM((tm, D_MODEL), BF16)],
        compiler_params=_params("parallel", "arbitrary"),
        name="mem_kv",
    )(mem2, g, w)


def _mem_attn_kernel(q_ref, kv_ref, o_ref):
    scale = X_HEAD_DIM ** -0.5
    for h in range(X_HEADS):
        c = slice(h * X_HEAD_DIM, (h + 1) * X_HEAD_DIM)
        k = kv_ref[:, h * X_HEAD_DIM:(h + 1) * X_HEAD_DIM]
        v = kv_ref[:, X_WIDTH + h * X_HEAD_DIM:X_WIDTH + (h + 1) * X_HEAD_DIM]
        s = lax.dot_general(q_ref[:, c], k, (((1,), (1,)), ((), ())), preferred_element_type=F32) * scale
        m = jnp.max(s, axis=-1, keepdims=True)
        p = jnp.exp(s - m)
        l = jnp.sum(p, axis=-1, keepdims=True)
        o = jnp.dot(p.astype(BF16), v, preferred_element_type=F32)
        o_ref[:, c] = (o / l).astype(BF16)


def _mem_attn(z, kvm, *, seq, mem_len, tm=1024):
    T = z.shape[0]
    return pl.pallas_call(
        _mem_attn_kernel,
        grid=(T // tm,),
        in_specs=[
            pl.BlockSpec((tm, X_WIDTH), lambda i: (i, Z_QX // X_WIDTH)),
            pl.BlockSpec((mem_len, 2 * X_WIDTH), lambda i: (i // (seq // tm), 0)),
        ],
        out_specs=pl.BlockSpec((tm, X_WIDTH), lambda i: (i, 0)),
        out_shape=jax.ShapeDtypeStruct((T, X_WIDTH), BF16),
        compiler_params=_params("parallel"),
        name="mem_attn",
    )(z, kvm)


def _merge_kernel(o_ref, cv_ref, cb_ref, cc_ref, cvp_ref, ccp_ref, cvn_ref, ccn_ref, yx_ref,
                  g0_ref, g1_ref, g2_ref, b_ref, cw_ref,
                  wa_ref, wc_ref, wm_ref, out_ref, pext_ref, *, tm, tn, tiles_per_seq):
    pos_in_seq = pl.program_id(0) % tiles_per_seq
    keep_prev = jnp.where(pos_in_seq == 0, 0.0, 1.0)
    keep_next = jnp.where(pos_in_seq == tiles_per_seq - 1, 0.0, 1.0)
    h8 = SUBLANES_F32
    prev = (cvp_ref[h8:, :].astype(F32) * ccp_ref[h8:, :].astype(F32)) * keep_prev
    nxt = (cvn_ref[:h8, :].astype(F32) * ccn_ref[:h8, :].astype(F32)) * keep_next
    pext_ref[0:h8, :] = prev
    pext_ref[tm + h8:tm + 2 * h8, :] = nxt
    pext_ref[h8:tm + h8, :] = cv_ref[...].astype(F32) * cc_ref[...].astype(F32)
    conv = (cw_ref[0:1, :] * pext_ref[h8 - 1:tm + h8 - 1, :] + cw_ref[1:2, :] * pext_ref[h8:tm + h8, :]
            + cw_ref[2:3, :] * pext_ref[h8 + 1:tm + h8 + 1, :])
    cbr = (cb_ref[...].astype(F32) * conv).astype(BF16)

    def gate(g_ref, k, c):
        return jax.nn.sigmoid(g_ref[:, c].astype(F32) + b_ref[:, k * D_MODEL + c.start:k * D_MODEL + c.stop])

    for n in range(D_MODEL // tn):
        c = slice(n * tn, (n + 1) * tn)
        y = gate(g0_ref, 0, c) * jnp.dot(o_ref[...], wa_ref[:, c], preferred_element_type=F32)
        y += gate(g1_ref, 1, c) * jnp.dot(cbr, wc_ref[:, c], preferred_element_type=F32)
        y += gate(g2_ref, 2, c) * jnp.dot(yx_ref[...], wm_ref[:, c], preferred_element_type=F32)
        out_ref[:, c] = y.astype(BF16)


def _merge(o_mla, z, yx, gate_bias, conv_w, w_o_mla, w_out_conv, w_o_mem, *, seq, tm=512, tn=1024):
    T = z.shape[0]
    hb = SUBLANES_BF16
    rb = tm // hb
    last_hb = T // hb - 1
    cw = CONV_WIDTH
    once = pl.Buffered(1)

    def zcol(c0, width):
        return c0 // width

    def gate_spec(k):
        return pl.BlockSpec((tm, D_MODEL), lambda i, k=k: (i, Z_G // D_MODEL + k))

    def prev_spec(c0):
        return pl.BlockSpec((hb, cw), lambda i: (jnp.maximum(i * rb - 1, 0), zcol(c0, cw)))

    def next_spec(c0):
        return pl.BlockSpec((hb, cw), lambda i: (jnp.minimum((i + 1) * rb, last_hb), zcol(c0, cw)))

    def w_spec():
        return pl.BlockSpec((cw, D_MODEL), lambda i: (0, 0), pipeline_mode=once)

    assert Z_G % D_MODEL == 0
    return pl.pallas_call(
        functools.partial(_merge_kernel, tm=tm, tn=tn, tiles_per_seq=seq // tm),
        grid=(T // tm,),
        in_specs=[
            pl.BlockSpec((tm, MLA_HEADS * V_HEAD), lambda i: (i, 0)),
            pl.BlockSpec((tm, cw), lambda i: (i, zcol(Z_CV, cw))),
            pl.BlockSpec((tm, cw), lambda i: (i, zcol(Z_CB, cw))),
            pl.BlockSpec((tm, cw), lambda i: (i, zcol(Z_CC, cw))),
            prev_spec(Z_CV), prev_spec(Z_CC), next_spec(Z_CV), next_spec(Z_CC),
            pl.BlockSpec((tm, X_WIDTH), lambda i: (i, 0)),
            gate_spec(0), gate_spec(1), gate_spec(2),
            pl.BlockSpec((1, N_BRANCH * D_MODEL), lambda i: (0, 0), pipeline_mode=once),
            pl.BlockSpec((3, cw), lambda i: (0, 0), pipeline_mode=once),
            w_spec(), w_spec(), w_spec(),
        ],
        out_specs=pl.BlockSpec((tm, D_MODEL), lambda i: (i, 0)),
        out_shape=jax.ShapeDtypeStruct((T, D_MODEL), BF16),
        scratch_shapes=[pltpu.VMEM((tm + 2 * SUBLANES_F32, cw), F32)],
        compiler_params=_params("parallel"),
        name="merge",
    )(o_mla, z, z, z, z, z, z, z, yx, z, z, z, gate_bias, conv_w, w_o_mla, w_out_conv, w_o_mem)


def _out_proj_kernel(m_ref, w_ref, x_ref, o_ref):
    o_ref[...] = x_ref[...] + jnp.dot(m_ref[...], w_ref[...], preferred_element_type=F32)


def _out_proj(merged, w_o, x2, *, tm=1024, tn=1024):
    T = x2.shape[0]
    return pl.pallas_call(
        _out_proj_kernel,
        grid=(T // tm, D_MODEL // tn),
        in_specs=[
            pl.BlockSpec((tm, D_MODEL), lambda i, j: (i, 0)),
            pl.BlockSpec((D_MODEL, tn), lambda i, j: (0, j)),
            pl.BlockSpec((tm, tn), lambda i, j: (i, j)),
        ],
        out_specs=pl.BlockSpec((tm, tn), lambda i, j: (i, j)),
        out_shape=jax.ShapeDtypeStruct((T, D_MODEL), F32),
        compiler_params=_params("parallel", "arbitrary"),
        name="out_proj",
    )(merged, w_o, x2)


FFN_COL_CHUNK = 256


def _ffn_kernel(x_ref, xp_ref, xn_ref, g_ref, wu_ref, ct_ref, wd_ref, fg_ref,
                o_ref, hn_ref, ua_ref, ub_ref, acc_ref, *, tm, tf, tiles_per_seq, n_f, final):
    i = pl.program_id(0)
    s = pl.program_id(1)
    h8 = SUBLANES_F32
    cw = FFN_COL_CHUNK
    rows = 128
    n_chunks = tf // cw
    n_pairs = n_f // 2
    n_steps = n_pairs + n_f % 2

    @pl.when(s == 0)
    def _():
        pos_in_seq = i % tiles_per_seq
        keep_prev = jnp.where(pos_in_seq == 0, 0.0, 1.0)
        keep_next = jnp.where(pos_in_seq == tiles_per_seq - 1, 0.0, 1.0)
        halo = jnp.concatenate([_rms(xp_ref[...], g_ref[...]) * keep_prev,
                                _rms(xn_ref[...], g_ref[...]) * keep_next], axis=0)
        hn_ref[tm:tm + 2 * h8, :] = halo.astype(BF16)
        _rms_rows_to(x_ref, g_ref, hn_ref, tm, rows)
        acc_ref[...] = jnp.zeros_like(acc_ref)

    def up(fb, c0, u_ref):
        r = jnp.dot(hn_ref[...], wu_ref[fb, :, c0:c0 + cw], preferred_element_type=F32)
        u_ref[h8:tm + h8, :] = r[:tm]
        u_ref[0:h8, :] = r[tm:tm + h8]
        u_ref[tm + h8:tm + 2 * h8, :] = r[tm + h8:]

    def conv(u_ref, fb, c0):
        return (ct_ref[fb, 0:1, c0:c0 + cw] * u_ref[h8 - 1:tm + h8 - 1, :]
                + ct_ref[fb, 1:2, c0:c0 + cw] * u_ref[h8:tm + h8, :]
                + ct_ref[fb, 2:3, c0:c0 + cw] * u_ref[h8 + 1:tm + h8 + 1, :])

    def block(fb):
        down = None
        for n in range(n_chunks):
            c0 = n * cw
            slot = fb * n_chunks + n
            up(fb, c0, ua_ref.at[slot])
            up(fb, tf + c0, ub_ref.at[slot])
            a = conv(ua_ref.at[slot], fb, c0)
            b = conv(ub_ref.at[slot], fb, tf + c0)
            act = (a * jax.nn.sigmoid(a) * b).astype(BF16)
            d = jnp.dot(act, wd_ref[fb * tf + c0:fb * tf + c0 + cw, :], preferred_element_type=F32)
            down = d if down is None else down + d
        acc_ref[...] += down

    @pl.when(s < n_pairs)
    def _():
        block(0)
        block(1)

    if n_f % 2:
        pl.when(s == n_pairs)(functools.partial(block, 0))

    @pl.when(s == n_steps - 1)
    def _():
        def body(c, carry):
            r = pl.ds(pl.multiple_of(c * rows, rows), rows)
            y = x_ref[r, :] + acc_ref[r, :]
            o_ref[r, :] = _rms(y, fg_ref[...]) if final else y
            return carry
        lax.fori_loop(0, tm // rows, body, 0)


def _ffn(x1, ffn_norm, w_up_blk, conv_taps, w_down, final_norm, *, seq, final, tm=512):
    T = x1.shape[0]
    h8 = SUBLANES_F32
    cw = FFN_COL_CHUNK
    rb = tm // h8
    last_hb = T // h8 - 1
    n_f = w_up_blk.shape[0]
    tf = w_up_blk.shape[2] // 2
    n_steps = (n_f + 1) // 2
    return pl.pallas_call(
        functools.partial(_ffn_kernel, tm=tm, tf=tf, tiles_per_seq=seq // tm, n_f=n_f, final=final),
        grid=(T // tm, n_steps),
        in_specs=[
            pl.BlockSpec((tm, D_MODEL), lambda i, s: (i, 0)),
            pl.BlockSpec((h8, D_MODEL), lambda i, s: (jnp.maximum(i * rb - 1, 0), 0)),
            pl.BlockSpec((h8, D_MODEL), lambda i, s: (jnp.minimum((i + 1) * rb, last_hb), 0)),
            pl.BlockSpec((1, D_MODEL), lambda i, s: (0, 0)),
            pl.BlockSpec((2, D_MODEL, 2 * tf), lambda i, s: (s, 0, 0)),
            pl.BlockSpec((2, 3, 2 * tf), lambda i, s: (s, 0, 0)),
            pl.BlockSpec((2 * tf, D_MODEL), lambda i, s: (s, 0)),
            pl.BlockSpec((1, D_MODEL), lambda i, s: (0, 0)),
        ],
        out_specs=pl.BlockSpec((tm, D_MODEL), lambda i, s: (i, 0)),
        out_shape=jax.ShapeDtypeStruct((T, D_MODEL), F32),
        scratch_shapes=[
            pltpu.VMEM((tm + 2 * h8, D_MODEL), BF16),
            pltpu.VMEM((2 * tf // cw, tm + 2 * h8, cw), F32),
            pltpu.VMEM((2 * tf // cw, tm + 2 * h8, cw), F32),
            pltpu.VMEM((tm, D_MODEL), F32),
        ],
        compiler_params=_params("parallel", "arbitrary"),
        name="ffn",
    )(x1, x1, x1, ffn_norm, w_up_blk, conv_taps, w_down, final_norm)


def _block_conv_taps(ffn_conv_w):
    k = ffn_conv_w.shape[0]
    n_f = D_FF // FFN_TF
    halves = ffn_conv_w.reshape(k, 2, n_f, FFN_TF)
    return jnp.transpose(halves, (2, 0, 1, 3)).reshape(n_f, k, 2 * FFN_TF)


def _pack_w_uq(w_uq):
    w = w_uq.reshape(Q_LORA, MLA_HEADS, QK_NOPE + QK_ROPE)
    nope = w[:, :, :QK_NOPE].reshape(Q_LORA, MLA_HEADS * QK_NOPE)
    rope = jnp.pad(w[:, :, QK_NOPE:], ((0, 0), (0, 0), (0, LANES - QK_ROPE))).reshape(Q_LORA, MLA_HEADS * LANES)
    return jnp.concatenate([nope, rope], axis=1).astype(BF16)


def kernel(x, mem, positions, mix_norm, w_in, q_norm, w_uq, kv_norm, w_ukv, w_o_mla, conv_w, w_out_conv,
           mem_norm, w_mem_kv, w_o_mem, gate_bias, w_o, ffn_norm, w_up, ffn_conv_w, w_down, final_norm):
    B, S, D = x.shape
    M = mem.shape[1]
    T = B * S
    depth = w_in.shape[0]
    x2 = x.reshape(T, D)
    mem2 = mem.reshape(B * M, D)
    pos = positions.reshape(T, 1)
    inv_freq = jnp.power(ROPE_THETA, -jnp.arange(0, QK_ROPE, 2, dtype=F32) / QK_ROPE)
    invf = jnp.concatenate([inv_freq, inv_freq, jnp.zeros((LANES - QK_ROPE,), F32)]).reshape(1, LANES)

    for l in range(depth):
        w_blk, w_kr = _pack_w_in(jnp.swapaxes(w_in[l], 0, 1))
        z, kr = _in_proj(x2, mix_norm[l].reshape(1, D), w_blk, w_kr)
        q, k, v = _qkv(z, kr, pos, invf, q_norm[l].reshape(1, -1), kv_norm[l].reshape(1, -1),
                       _pack_w_uq(w_uq[l]), w_ukv[l].astype(BF16))
        o_mla = _mla(q, k, v, batch=B, seq=S)
        kvm = _mem_kv(mem2, mem_norm[l].reshape(1, D), w_mem_kv[l].astype(BF16))
        yx = _mem_attn(z, kvm, seq=S, mem_len=M)
        merged = _merge(o_mla, z, yx, gate_bias[l].reshape(1, -1), conv_w[l],
                        w_o_mla[l].astype(BF16), w_out_conv[l].astype(BF16), w_o_mem[l].astype(BF16), seq=S)
        x2 = _out_proj(merged, w_o[l].astype(BF16), x2)
        x2 = _ffn(x2, ffn_norm[l].reshape(1, D), _pack_w_up(w_up[l]), _block_conv_taps(ffn_conv_w[l]),
                  w_down[l].astype(BF16), final_norm.reshape(1, D), seq=S, final=(l == depth - 1))
    return x2.reshape(B, S, D)
```

```python
import functools

import jax
import jax.numpy as jnp
from jax import lax
from jax.experimental import pallas as pl
from jax.experimental.pallas import tpu as pltpu

F32 = jnp.float32
BF16 = jnp.bfloat16

D_MODEL = 2048
MLA_HEADS = 8
Q_LORA = 512
KV_LORA = 512
QK_NOPE = 128
QK_ROPE = 64
V_HEAD = 128
ROPE_THETA = 10000.0
CONV_WIDTH = 1024
X_HEADS = 4
X_HEAD_DIM = 256
X_WIDTH = X_HEADS * X_HEAD_DIM
D_FF = 5632
N_BRANCH = 3
EPS = 1e-6
LOG2_E = 1.4426950408889634

LANES = 128
SUBLANES_F32 = 8
SUBLANES_BF16 = 16
HEAD_PAD = 256
VMEM_LIMIT = 56 * 1024 * 1024

Z_G = 0
Z_CQ = Z_G + N_BRANCH * D_MODEL
Z_CKV = Z_CQ + Q_LORA
Z_CV = Z_CKV + KV_LORA
Z_CB = Z_CV + CONV_WIDTH
Z_CC = Z_CB + CONV_WIDTH
Z_QX = Z_CC + CONV_WIDTH
Z_COLS = Z_QX + X_WIDTH


_NT = (((1,), (1,)), ((), ()))


def _params(*sem):
    return pltpu.CompilerParams(dimension_semantics=sem, vmem_limit_bytes=VMEM_LIMIT)


def _rms(x, g):
    inv = lax.rsqrt(jnp.mean(x * x, axis=-1, keepdims=True) + EPS)
    return x * inv * g


def _rms_rows_to(x_ref, g_ref, out_ref, rows, chunk, out_row0=0):
    def body(c, carry):
        r0 = pl.multiple_of(c * chunk, chunk)
        x = x_ref[pl.ds(r0, chunk), :].astype(F32)
        out_ref[pl.ds(out_row0 + r0, chunk), :] = _rms(x, g_ref[...]).astype(out_ref.dtype)
        return carry
    lax.fori_loop(0, rows // chunk, body, 0)


IN_TN = 1024
FFN_TF = 512


N_GATE_BLK = N_BRANCH * D_MODEL // IN_TN


def _pack_w_in_kernel(a_ref, b_ref, kr_ref, wblk_ref, wkr_ref):
    j = pl.program_id(0)

    @pl.when(j == N_GATE_BLK)
    def _():
        wblk_ref[0] = a_ref[...].astype(BF16)

    @pl.when(j != N_GATE_BLK)
    def _():
        wblk_ref[0, :IN_TN - QK_ROPE, :] = a_ref[QK_ROPE:, :].astype(BF16)
        wblk_ref[0, IN_TN - QK_ROPE:, :] = b_ref[...].astype(BF16)

    @pl.when(j == 0)
    def _():
        wkr_ref[:QK_ROPE, :] = kr_ref[...].astype(BF16)
        wkr_ref[QK_ROPE:, :] = jnp.zeros((LANES - QK_ROPE, wkr_ref.shape[1]), BF16)


def _pack_w_in(wi_t):
    cols, d = wi_t.shape
    kr0 = Q_LORA + KV_LORA
    assert cols == Z_COLS + QK_ROPE and kr0 == IN_TN and IN_TN % QK_ROPE == 0
    n_blk = Z_COLS // IN_TN
    n_tail = n_blk - 1 - N_GATE_BLK
    sub = IN_TN // QK_ROPE

    def src_blk(j):
        return jnp.where(j < N_GATE_BLK, j + 1 + n_tail, jnp.where(j == N_GATE_BLK, 0, j - N_GATE_BLK))

    return pl.pallas_call(
        _pack_w_in_kernel,
        grid=(n_blk,),
        in_specs=[
            pl.BlockSpec((IN_TN, d), lambda j: (src_blk(j), 0)),
            pl.BlockSpec((QK_ROPE, d), lambda j: ((src_blk(j) + 1) * sub, 0)),
            pl.BlockSpec((QK_ROPE, d), lambda j: (kr0 // QK_ROPE, 0)),
        ],
        out_specs=[
            pl.BlockSpec((1, IN_TN, d), lambda j: (j, 0, 0)),
            pl.BlockSpec((LANES, d), lambda j: (0, 0)),
        ],
        out_shape=[
            jax.ShapeDtypeStruct((n_blk, IN_TN, d), BF16),
            jax.ShapeDtypeStruct((LANES, d), BF16),
        ],
        compiler_params=_params("arbitrary"),
        name="pack_w_in",
    )(wi_t, wi_t, wi_t)


def _pack_w_up_kernel(w_ref, o_ref):
    o_ref[0] = w_ref[...].astype(BF16)


def _pack_w_up(w_up):
    d = w_up.shape[0]
    n_f = D_FF // FFN_TF
    return pl.pallas_call(
        _pack_w_up_kernel,
        grid=(n_f, 2),
        in_specs=[pl.BlockSpec((d, FFN_TF), lambda f, h: (0, h * n_f + f))],
        out_specs=pl.BlockSpec((1, d, FFN_TF), lambda f, h: (f, 0, h)),
        out_shape=jax.ShapeDtypeStruct((n_f, d, 2 * FFN_TF), BF16),
        compiler_params=_params("parallel", "parallel"),
        name="pack_w_up",
    )(w_up)


def _in_proj_kernel(x_ref, g_ref, w_ref, wkr_ref, z_ref, kr_ref, h_ref, *, tm):
    @pl.when(pl.program_id(1) == 0)
    def _():
        _rms_rows_to(x_ref, g_ref, h_ref, tm, 128)
        kr_ref[...] = lax.dot_general(h_ref[...], wkr_ref[...], _NT, preferred_element_type=F32)

    z_ref[...] = lax.dot_general(h_ref[...], w_ref[0], _NT, preferred_element_type=F32).astype(BF16)


def _in_proj(x2, g, w_blk, w_kr, *, tm=1024):
    T = x2.shape[0]
    tn = IN_TN
    return pl.pallas_call(
        functools.partial(_in_proj_kernel, tm=tm),
        grid=(T // tm, Z_COLS // tn),
        in_specs=[
            pl.BlockSpec((tm, D_MODEL), lambda i, j: (i, 0)),
            pl.BlockSpec((1, D_MODEL), lambda i, j: (0, 0)),
            pl.BlockSpec((1, tn, D_MODEL), lambda i, j: (j, 0, 0)),
            pl.BlockSpec((LANES, D_MODEL), lambda i, j: (0, 0)),
        ],
        out_specs=[
            pl.BlockSpec((tm, tn), lambda i, j: (i, j)),
            pl.BlockSpec((tm, LANES), lambda i, j: (i, 0)),
        ],
        out_shape=[
            jax.ShapeDtypeStruct((T, Z_COLS), BF16),
            jax.ShapeDtypeStruct((T, LANES), F32),
        ],
        scratch_shapes=[pltpu.VMEM((tm, D_MODEL), BF16)],
        compiler_params=_params("parallel", "arbitrary"),
        name="in_proj",
    )(x2, g, w_blk, w_kr)


def _qkv_kernel(cq_ref, ckv_ref, kr_ref, pos_ref, invf_ref, qn_ref, kvn_ref, wuq_ref, wukv_ref,
                q_ref, k_ref, v_ref, *, tm):
    ang = pos_ref[...].astype(F32) * invf_ref[...]
    cos = jnp.cos(ang)
    sin = jnp.sin(ang)
    lane = lax.broadcasted_iota(jnp.int32, (tm, LANES), 1)
    half = QK_ROPE // 2
    c_tab = jnp.where(lane < QK_ROPE, cos, 0.0)
    s_lo = jnp.where(lane < half, -sin, 0.0)
    s_hi = jnp.where((lane >= half) & (lane < QK_ROPE), sin, 0.0)

    def rope(t):
        return t * c_tab + pltpu.roll(t, LANES - half, 1) * s_lo + pltpu.roll(t, half, 1) * s_hi

    scale = (QK_NOPE + QK_ROPE) ** -0.5 * LOG2_E
    cqn = _rms(cq_ref[...].astype(F32), qn_ref[...]).astype(BF16)
    q = jnp.dot(cqn, wuq_ref[...], preferred_element_type=F32)
    nope_cols = MLA_HEADS * QK_NOPE
    for h in range(MLA_HEADS):
        q_ref[:, h * HEAD_PAD:h * HEAD_PAD + QK_NOPE] = (
            q[:, h * QK_NOPE:(h + 1) * QK_NOPE] * scale).astype(BF16)
        q_ref[:, h * HEAD_PAD + QK_NOPE:(h + 1) * HEAD_PAD] = (
            rope(q[:, nope_cols + h * LANES:nope_cols + (h + 1) * LANES]) * scale).astype(BF16)

    ckvn = _rms(ckv_ref[...].astype(F32), kvn_ref[...]).astype(BF16)
    kv = jnp.dot(ckvn, wukv_ref[...], preferred_element_type=F32)
    k_rope = rope(kr_ref[...]).astype(BF16)
    ones = jnp.ones((tm, V_HEAD), BF16)
    for h in range(MLA_HEADS):
        k_ref[:, h * HEAD_PAD:h * HEAD_PAD + QK_NOPE] = kv[:, h * HEAD_PAD:h * HEAD_PAD + QK_NOPE].astype(BF16)
        k_ref[:, h * HEAD_PAD + QK_NOPE:(h + 1) * HEAD_PAD] = k_rope
        v_ref[:, 2 * h * V_HEAD:(2 * h + 1) * V_HEAD] = kv[:, h * HEAD_PAD + QK_NOPE:(h + 1) * HEAD_PAD].astype(BF16)
        v_ref[:, (2 * h + 1) * V_HEAD:(2 * h + 2) * V_HEAD] = ones


def _qkv(z, kr, pos, invf, q_norm, kv_norm, w_uq_p, w_ukv, *, tm=512):
    T = z.shape[0]
    qk_cols = MLA_HEADS * HEAD_PAD
    return pl.pallas_call(
        functools.partial(_qkv_kernel, tm=tm),
        grid=(T // tm,),
        in_specs=[
            pl.BlockSpec((tm, Q_LORA), lambda i: (i, Z_CQ // Q_LORA)),
            pl.BlockSpec((tm, KV_LORA), lambda i: (i, Z_CKV // KV_LORA)),
            pl.BlockSpec((tm, LANES), lambda i: (i, 0)),
            pl.BlockSpec((tm, 1), lambda i: (i, 0)),
            pl.BlockSpec((1, LANES), lambda i: (0, 0)),
            pl.BlockSpec((1, Q_LORA), lambda i: (0, 0)),
            pl.BlockSpec((1, KV_LORA), lambda i: (0, 0)),
            pl.BlockSpec((Q_LORA, qk_cols), lambda i: (0, 0)),
            pl.BlockSpec((KV_LORA, qk_cols), lambda i: (0, 0)),
        ],
        out_specs=[
            pl.BlockSpec((tm, qk_cols), lambda i: (i, 0)),
            pl.BlockSpec((tm, qk_cols), lambda i: (i, 0)),
            pl.BlockSpec((tm, 2 * MLA_HEADS * V_HEAD), lambda i: (i, 0)),
        ],
        out_shape=[
            jax.ShapeDtypeStruct((T, qk_cols), BF16),
            jax.ShapeDtypeStruct((T, qk_cols), BF16),
            jax.ShapeDtypeStruct((T, 2 * MLA_HEADS * V_HEAD), BF16),
        ],
        compiler_params=_params("parallel"),
        name="qkv",
    )(z, z, kr, pos, invf, q_norm, kv_norm, w_uq_p, w_ukv)


def _mla_kernel(q_ref, k_ref, v_ref, o_ref, *, seq, tq, heads):
    for h in range(heads):
        k = k_ref[:, h * HEAD_PAD:(h + 1) * HEAD_PAD]
        v = v_ref[:, 2 * h * V_HEAD:2 * (h + 1) * V_HEAD]
        for c in range(seq // tq):
            r = slice(c * tq, (c + 1) * tq)
            s = lax.dot_general(q_ref[r, h * HEAD_PAD:(h + 1) * HEAD_PAD], k, (((1,), (1,)), ((), ())),
                                preferred_element_type=F32)
            m = jnp.max(s, axis=-1, keepdims=True)
            p = jnp.exp2(s - m).astype(BF16)
            ov = jnp.dot(p, v, preferred_element_type=F32)
            o_ref[r, h * V_HEAD:(h + 1) * V_HEAD] = (ov[:, :V_HEAD] / ov[:, V_HEAD:]).astype(BF16)


def _mla(q, k, v, *, batch, seq, tq=512, heads=4):
    T = q.shape[0]
    return pl.pallas_call(
        functools.partial(_mla_kernel, seq=seq, tq=tq, heads=heads),
        grid=(batch, MLA_HEADS // heads),
        in_specs=[
            pl.BlockSpec((seq, heads * HEAD_PAD), lambda b, h: (b, h)),
            pl.BlockSpec((seq, heads * HEAD_PAD), lambda b, h: (b, h)),
            pl.BlockSpec((seq, heads * 2 * V_HEAD), lambda b, h: (b, h)),
        ],
        out_specs=pl.BlockSpec((seq, heads * V_HEAD), lambda b, h: (b, h)),
        out_shape=jax.ShapeDtypeStruct((T, MLA_HEADS * V_HEAD), BF16),
        compiler_params=_params("parallel", "parallel"),
        name="mla_attn",
    )(q, k, v)


def _mem_kv_kernel(m_ref, g_ref, w_ref, o_ref, h_ref, *, tm):
    @pl.when(pl.program_id(1) == 0)
    def _():
        _rms_rows_to(m_ref, g_ref, h_ref, tm, 128)

    o_ref[...] = jnp.dot(h_ref[...], w_ref[...], preferred_element_type=F32).astype(BF16)


def _mem_kv(mem2, g, w, *, tm=1024, tn=1024):
    R = mem2.shape[0]
    N = w.shape[1]
    return pl.pallas_call(
        functools.partial(_mem_kv_kernel, tm=tm),
        grid=(R // tm, N // tn),
        in_specs=[
            pl.BlockSpec((tm, D_MODEL), lambda i, j: (i, 0)),
            pl.BlockSpec((1, D_MODEL), lambda i, j: (0, 0)),
            pl.BlockSpec((D_MODEL, tn), lambda i, j: (0, j)),
        ],
        out_specs=pl.BlockSpec((tm, tn), lambda i, j: (i, j)),
        out_shape=jax.ShapeDtypeStruct((R, N), BF16),
        scratch_shapes=[pltpu.VMEM((tm, D_MODEL), BF16)],
        compiler_params=_params("parallel", "arbitrary"),
        name="mem_kv",
    )(mem2, g, w)


def _mem_attn_kernel(q_ref, kv_ref, o_ref):
    scale = X_HEAD_DIM ** -0.5
    for h in range(X_HEADS):
        c = slice(h * X_HEAD_DIM, (h + 1) * X_HEAD_DIM)
        k = kv_ref[:, h * X_HEAD_DIM:(h + 1) * X_HEAD_DIM]
        v = kv_ref[:, X_WIDTH + h * X_HEAD_DIM:X_WIDTH + (h + 1) * X_HEAD_DIM]
        s = lax.dot_general(q_ref[:, c], k, (((1,), (1,)), ((), ())), preferred_element_type=F32) * scale
        m = jnp.max(s, axis=-1, keepdims=True)
        p = jnp.exp(s - m)
        l = jnp.sum(p, axis=-1, keepdims=True)
        o = jnp.dot(p.astype(BF16), v, preferred_element_type=F32)
        o_ref[:, c] = (o / l).astype(BF16)


def _mem_attn(z, kvm, *, seq, mem_len, tm=1024):
    T = z.shape[0]
    return pl.pallas_call(
        _mem_attn_kernel,
        grid=(T // tm,),
        in_specs=[
            pl.BlockSpec((tm, X_WIDTH), lambda i: (i, Z_QX // X_WIDTH)),
            pl.BlockSpec((mem_len, 2 * X_WIDTH), lambda i: (i // (seq // tm), 0)),
        ],
        out_specs=pl.BlockSpec((tm, X_WIDTH), lambda i: (i, 0)),
        out_shape=jax.ShapeDtypeStruct((T, X_WIDTH), BF16),
        compiler_params=_params("parallel"),
        name="mem_attn",
    )(z, kvm)


def _merge_kernel(o_ref, cv_ref, cb_ref, cc_ref, cvp_ref, ccp_ref, cvn_ref, ccn_ref, yx_ref,
                  g0_ref, g1_ref, g2_ref, b_ref, cw_ref,
                  wa_ref, wc_ref, wm_ref, out_ref, pext_ref, *, tm, tn, tiles_per_seq):
    pos_in_seq = pl.program_id(0) % tiles_per_seq
    keep_prev = jnp.where(pos_in_seq == 0, 0.0, 1.0)
    keep_next = jnp.where(pos_in_seq == tiles_per_seq - 1, 0.0, 1.0)
    h8 = SUBLANES_F32
    prev = (cvp_ref[h8:, :].astype(F32) * ccp_ref[h8:, :].astype(F32)) * keep_prev
    nxt = (cvn_ref[:h8, :].astype(F32) * ccn_ref[:h8, :].astype(F32)) * keep_next
    pext_ref[0:h8, :] = prev
    pext_ref[tm + h8:tm + 2 * h8, :] = nxt
    pext_ref[h8:tm + h8, :] = cv_ref[...].astype(F32) * cc_ref[...].astype(F32)
    conv = (cw_ref[0:1, :] * pext_ref[h8 - 1:tm + h8 - 1, :] + cw_ref[1:2, :] * pext_ref[h8:tm + h8, :]
            + cw_ref[2:3, :] * pext_ref[h8 + 1:tm + h8 + 1, :])
    cbr = (cb_ref[...].astype(F32) * conv).astype(BF16)

    def gate(g_ref, k, c):
        return jax.nn.sigmoid(g_ref[:, c].astype(F32) + b_ref[:, k * D_MODEL + c.start:k * D_MODEL + c.stop])

    for n in range(D_MODEL // tn):
        c = slice(n * tn, (n + 1) * tn)
        y = gate(g0_ref, 0, c) * jnp.dot(o_ref[...], wa_ref[:, c], preferred_element_type=F32)
        y += gate(g1_ref, 1, c) * jnp.dot(cbr, wc_ref[:, c], preferred_element_type=F32)
        y += gate(g2_ref, 2, c) * jnp.dot(yx_ref[...], wm_ref[:, c], preferred_element_type=F32)
        out_ref[:, c] = y.astype(BF16)


def _merge(o_mla, z, yx, gate_bias, conv_w, w_o_mla, w_out_conv, w_o_mem, *, seq, tm=512, tn=2048):
    T = z.shape[0]
    hb = SUBLANES_BF16
    rb = tm // hb
    last_hb = T // hb - 1
    cw = CONV_WIDTH
    once = pl.Buffered(1)

    def zcol(c0, width):
        return c0 // width

    def gate_spec(k):
        return pl.BlockSpec((tm, D_MODEL), lambda i, k=k: (i, Z_G // D_MODEL + k))

    def prev_spec(c0):
        return pl.BlockSpec((hb, cw), lambda i: (jnp.maximum(i * rb - 1, 0), zcol(c0, cw)))

    def next_spec(c0):
        return pl.BlockSpec((hb, cw), lambda i: (jnp.minimum((i + 1) * rb, last_hb), zcol(c0, cw)))

    def w_spec():
        return pl.BlockSpec((cw, D_MODEL), lambda i: (0, 0), pipeline_mode=once)

    assert Z_G % D_MODEL == 0
    return pl.pallas_call(
        functools.partial(_merge_kernel, tm=tm, tn=tn, tiles_per_seq=seq // tm),
        grid=(T // tm,),
        in_specs=[
            pl.BlockSpec((tm, MLA_HEADS * V_HEAD), lambda i: (i, 0)),
            pl.BlockSpec((tm, cw), lambda i: (i, zcol(Z_CV, cw))),
            pl.BlockSpec((tm, cw), lambda i: (i, zcol(Z_CB, cw))),
            pl.BlockSpec((tm, cw), lambda i: (i, zcol(Z_CC, cw))),
            prev_spec(Z_CV), prev_spec(Z_CC), next_spec(Z_CV), next_spec(Z_CC),
            pl.BlockSpec((tm, X_WIDTH), lambda i: (i, 0)),
            gate_spec(0), gate_spec(1), gate_spec(2),
            pl.BlockSpec((1, N_BRANCH * D_MODEL), lambda i: (0, 0), pipeline_mode=once),
            pl.BlockSpec((3, cw), lambda i: (0, 0), pipeline_mode=once),
            w_spec(), w_spec(), w_spec(),
        ],
        out_specs=pl.BlockSpec((tm, D_MODEL), lambda i: (i, 0)),
        out_shape=jax.ShapeDtypeStruct((T, D_MODEL), BF16),
        scratch_shapes=[pltpu.VMEM((tm + 2 * SUBLANES_F32, cw), F32)],
        compiler_params=_params("parallel"),
        name="merge",
    )(o_mla, z, z, z, z, z, z, z, yx, z, z, z, gate_bias, conv_w, w_o_mla, w_out_conv, w_o_mem)


def _out_proj_kernel(m_ref, w_ref, x_ref, o_ref):
    o_ref[...] = x_ref[...] + jnp.dot(m_ref[...], w_ref[...], preferred_element_type=F32)


def _out_proj(merged, w_o, x2, *, tm=1024, tn=1024):
    T = x2.shape[0]
    return pl.pallas_call(
        _out_proj_kernel,
        grid=(T // tm, D_MODEL // tn),
        in_specs=[
            pl.BlockSpec((tm, D_MODEL), lambda i, j: (i, 0)),
            pl.BlockSpec((D_MODEL, tn), lambda i, j: (0, j)),
            pl.BlockSpec((tm, tn), lambda i, j: (i, j)),
        ],
        out_specs=pl.BlockSpec((tm, tn), lambda i, j: (i, j)),
        out_shape=jax.ShapeDtypeStruct((T, D_MODEL), F32),
        compiler_params=_params("parallel", "arbitrary"),
        name="out_proj",
    )(merged, w_o, x2)


FFN_COL_CHUNK = 256


def _ffn_kernel(x_ref, xp_ref, xn_ref, g_ref, wu_ref, ct_ref, wd_ref, fg_ref,
                o_ref, hn_ref, u_ref, *, tm, tf, tiles_per_seq, n_f, final):
    acc_ref = o_ref
    i = pl.program_id(0)
    s = pl.program_id(1)
    h8 = SUBLANES_F32
    cw = FFN_COL_CHUNK
    rows = 128
    n_chunks = tf // cw
    n_pairs = n_f // 2
    n_steps = n_pairs + n_f % 2

    @pl.when(s == 0)
    def _():
        pos_in_seq = i % tiles_per_seq
        keep_prev = jnp.where(pos_in_seq == 0, 0.0, 1.0)
        keep_next = jnp.where(pos_in_seq == tiles_per_seq - 1, 0.0, 1.0)
        halo = jnp.concatenate([_rms(xp_ref[...], g_ref[...]) * keep_prev,
                                _rms(xn_ref[...], g_ref[...]) * keep_next], axis=0)
        hn_ref[tm:tm + 2 * h8, :] = halo.astype(BF16)
        _rms_rows_to(x_ref, g_ref, hn_ref, tm, rows)
        acc_ref[...] = jnp.zeros_like(acc_ref)

    def up(fb, u_ref):
        r = jnp.dot(hn_ref[...], wu_ref[fb], preferred_element_type=F32)
        u_ref[h8:tm + h8, :] = r[:tm]
        u_ref[0:h8, :] = r[tm:tm + h8]
        u_ref[tm + h8:tm + 2 * h8, :] = r[tm + h8:]

    def conv(u_ref, fb, c0):
        return (ct_ref[fb, 0:1, c0:c0 + cw] * u_ref[h8 - 1:tm + h8 - 1, c0:c0 + cw]
                + ct_ref[fb, 1:2, c0:c0 + cw] * u_ref[h8:tm + h8, c0:c0 + cw]
                + ct_ref[fb, 2:3, c0:c0 + cw] * u_ref[h8 + 1:tm + h8 + 1, c0:c0 + cw])

    def block(fb):
        up(fb, u_ref.at[fb])
        down = None
        for n in range(n_chunks):
            c0 = n * cw
            a = conv(u_ref.at[fb], fb, c0)
            b = conv(u_ref.at[fb], fb, tf + c0)
            act = (a * jax.nn.sigmoid(a) * b).astype(BF16)
            d = jnp.dot(act, wd_ref[fb * tf + c0:fb * tf + c0 + cw, :], preferred_element_type=F32)
            down = d if down is None else down + d
        acc_ref[...] += down

    @pl.when(s < n_pairs)
    def _():
        block(0)
        block(1)

    if n_f % 2:
        pl.when(s == n_pairs)(functools.partial(block, 0))

    @pl.when(s == n_steps - 1)
    def _():
        def body(c, carry):
            r = pl.ds(pl.multiple_of(c * rows, rows), rows)
            y = x_ref[r, :] + acc_ref[r, :]
            o_ref[r, :] = _rms(y, fg_ref[...]) if final else y
            return carry
        lax.fori_loop(0, tm // rows, body, 0)


def _ffn(x1, ffn_norm, w_up_blk, conv_taps, w_down, final_norm, *, seq, final, tm=512):
    T = x1.shape[0]
    h8 = SUBLANES_F32
    cw = FFN_COL_CHUNK
    rb = tm // h8
    last_hb = T // h8 - 1
    n_f = w_up_blk.shape[0]
    tf = w_up_blk.shape[2] // 2
    n_steps = (n_f + 1) // 2
    return pl.pallas_call(
        functools.partial(_ffn_kernel, tm=tm, tf=tf, tiles_per_seq=seq // tm, n_f=n_f, final=final),
        grid=(T // tm, n_steps),
        in_specs=[
            pl.BlockSpec((tm, D_MODEL), lambda i, s: (i, 0)),
            pl.BlockSpec((h8, D_MODEL), lambda i, s: (jnp.maximum(i * rb - 1, 0), 0)),
            pl.BlockSpec((h8, D_MODEL), lambda i, s: (jnp.minimum((i + 1) * rb, last_hb), 0)),
            pl.BlockSpec((1, D_MODEL), lambda i, s: (0, 0)),
            pl.BlockSpec((2, D_MODEL, 2 * tf), lambda i, s: (s, 0, 0)),
            pl.BlockSpec((2, 3, 2 * tf), lambda i, s: (s, 0, 0)),
            pl.BlockSpec((2 * tf, D_MODEL), lambda i, s: (s, 0)),
            pl.BlockSpec((1, D_MODEL), lambda i, s: (0, 0)),
        ],
        out_specs=pl.BlockSpec((tm, D_MODEL), lambda i, s: (i, 0)),
        out_shape=jax.ShapeDtypeStruct((T, D_MODEL), F32),
        scratch_shapes=[
            pltpu.VMEM((tm + 2 * h8, D_MODEL), BF16),
            pltpu.VMEM((2, tm + 2 * h8, 2 * tf), F32),
        ],
        compiler_params=_params("parallel", "arbitrary"),
        name="ffn",
    )(x1, x1, x1, ffn_norm, w_up_blk, conv_taps, w_down, final_norm)


def _block_conv_taps(ffn_conv_w):
    k = ffn_conv_w.shape[0]
    n_f = D_FF // FFN_TF
    halves = ffn_conv_w.reshape(k, 2, n_f, FFN_TF)
    return jnp.transpose(halves, (2, 0, 1, 3)).reshape(n_f, k, 2 * FFN_TF)


def _pack_w_uq(w_uq):
    w = w_uq.reshape(Q_LORA, MLA_HEADS, QK_NOPE + QK_ROPE)
    nope = w[:, :, :QK_NOPE].reshape(Q_LORA, MLA_HEADS * QK_NOPE)
    rope = jnp.pad(w[:, :, QK_NOPE:], ((0, 0), (0, 0), (0, LANES - QK_ROPE))).reshape(Q_LORA, MLA_HEADS * LANES)
    return jnp.concatenate([nope, rope], axis=1).astype(BF16)


def kernel(x, mem, positions, mix_norm, w_in, q_norm, w_uq, kv_norm, w_ukv, w_o_mla, conv_w, w_out_conv,
           mem_norm, w_mem_kv, w_o_mem, gate_bias, w_o, ffn_norm, w_up, ffn_conv_w, w_down, final_norm):
    B, S, D = x.shape
    M = mem.shape[1]
    T = B * S
    depth = w_in.shape[0]
    x2 = x.reshape(T, D)
    mem2 = mem.reshape(B * M, D)
    pos = positions.reshape(T, 1)
    inv_freq = jnp.power(ROPE_THETA, -jnp.arange(0, QK_ROPE, 2, dtype=F32) / QK_ROPE)
    invf = jnp.concatenate([inv_freq, inv_freq, jnp.zeros((LANES - QK_ROPE,), F32)]).reshape(1, LANES)

    for l in range(depth):
        w_blk, w_kr = _pack_w_in(jnp.swapaxes(w_in[l], 0, 1))
        z, kr = _in_proj(x2, mix_norm[l].reshape(1, D), w_blk, w_kr)
        q, k, v = _qkv(z, kr, pos, invf, q_norm[l].reshape(1, -1), kv_norm[l].reshape(1, -1),
                       _pack_w_uq(w_uq[l]), w_ukv[l].astype(BF16))
        o_mla = _mla(q, k, v, batch=B, seq=S)
        kvm = _mem_kv(mem2, mem_norm[l].reshape(1, D), w_mem_kv[l].astype(BF16))
        yx = _mem_attn(z, kvm, seq=S, mem_len=M)
        merged = _merge(o_mla, z, yx, gate_bias[l].reshape(1, -1), conv_w[l],
                        w_o_mla[l].astype(BF16), w_out_conv[l].astype(BF16), w_o_mem[l].astype(BF16), seq=S)
        x2 = _out_proj(merged, w_o[l].astype(BF16), x2)
        x2 = _ffn(x2, ffn_norm[l].reshape(1, D), _pack_w_up(w_up[l]), _block_conv_taps(ffn_conv_w[l]),
                  w_down[l].astype(BF16), final_norm.reshape(1, D), seq=S, final=(l == depth - 1))
    return x2.reshape(B, S, D)
```

```python
import functools

import jax
import jax.numpy as jnp
from jax import lax
from jax.experimental import pallas as pl
from jax.experimental.pallas import tpu as pltpu

F32 = jnp.float32
BF16 = jnp.bfloat16

D_MODEL = 2048
MLA_HEADS = 8
Q_LORA = 512
KV_LORA = 512
QK_NOPE = 128
QK_ROPE = 64
V_HEAD = 128
ROPE_THETA = 10000.0
CONV_WIDTH = 1024
X_HEADS = 4
X_HEAD_DIM = 256
X_WIDTH = X_HEADS * X_HEAD_DIM
D_FF = 5632
N_BRANCH = 3
EPS = 1e-6
LOG2_E = 1.4426950408889634
Q_SCALE = (QK_NOPE + QK_ROPE) ** -0.5 * LOG2_E

LANES = 128
SUBLANES_F32 = 8
SUBLANES_BF16 = 16
HEAD_PAD = 256
VMEM_LIMIT = 56 * 1024 * 1024

Z_G = 0
Z_CQ = Z_G + N_BRANCH * D_MODEL
Z_CKV = Z_CQ + Q_LORA
Z_CV = Z_CKV + KV_LORA
Z_CB = Z_CV + CONV_WIDTH
Z_CC = Z_CB + CONV_WIDTH
Z_QX = Z_CC + CONV_WIDTH
Z_COLS = Z_QX + X_WIDTH


_NT = (((1,), (1,)), ((), ()))


def _params(*sem):
    return pltpu.CompilerParams(dimension_semantics=sem, vmem_limit_bytes=VMEM_LIMIT)


def _rms(x, g):
    inv = lax.rsqrt(jnp.mean(x * x, axis=-1, keepdims=True) + EPS)
    return x * inv * g


def _rms_rows_to(x_ref, g_ref, out_ref, rows, chunk, out_row0=0):
    def body(c, carry):
        r0 = pl.multiple_of(c * chunk, chunk)
        x = x_ref[pl.ds(r0, chunk), :].astype(F32)
        out_ref[pl.ds(out_row0 + r0, chunk), :] = _rms(x, g_ref[...]).astype(out_ref.dtype)
        return carry
    lax.fori_loop(0, rows // chunk, body, 0)


IN_TN = 1024
FFN_TF = 512


N_GATE_BLK = N_BRANCH * D_MODEL // IN_TN


def _pack_w_in_kernel(a_ref, b_ref, kr_ref, wblk_ref, wkr_ref):
    j = pl.program_id(0)

    @pl.when(j == N_GATE_BLK)
    def _():
        wblk_ref[0] = a_ref[...].astype(BF16)

    @pl.when(j != N_GATE_BLK)
    def _():
        wblk_ref[0, :IN_TN - QK_ROPE, :] = a_ref[QK_ROPE:, :].astype(BF16)
        wblk_ref[0, IN_TN - QK_ROPE:, :] = b_ref[...].astype(BF16)

    @pl.when(j == 0)
    def _():
        wkr_ref[:QK_ROPE, :] = kr_ref[...].astype(BF16)
        wkr_ref[QK_ROPE:, :] = jnp.zeros((LANES - QK_ROPE, wkr_ref.shape[1]), BF16)


def _pack_w_in(wi_t):
    cols, d = wi_t.shape
    kr0 = Q_LORA + KV_LORA
    assert cols == Z_COLS + QK_ROPE and kr0 == IN_TN and IN_TN % QK_ROPE == 0
    n_blk = Z_COLS // IN_TN
    n_tail = n_blk - 1 - N_GATE_BLK
    sub = IN_TN // QK_ROPE

    def src_blk(j):
        return jnp.where(j < N_GATE_BLK, j + 1 + n_tail, jnp.where(j == N_GATE_BLK, 0, j - N_GATE_BLK))

    return pl.pallas_call(
        _pack_w_in_kernel,
        grid=(n_blk,),
        in_specs=[
            pl.BlockSpec((IN_TN, d), lambda j: (src_blk(j), 0)),
            pl.BlockSpec((QK_ROPE, d), lambda j: ((src_blk(j) + 1) * sub, 0)),
            pl.BlockSpec((QK_ROPE, d), lambda j: (kr0 // QK_ROPE, 0)),
        ],
        out_specs=[
            pl.BlockSpec((1, IN_TN, d), lambda j: (j, 0, 0)),
            pl.BlockSpec((LANES, d), lambda j: (0, 0)),
        ],
        out_shape=[
            jax.ShapeDtypeStruct((n_blk, IN_TN, d), BF16),
            jax.ShapeDtypeStruct((LANES, d), BF16),
        ],
        compiler_params=_params("arbitrary"),
        name="pack_w_in",
    )(wi_t, wi_t, wi_t)


def _pack_w_up_kernel(w_ref, o_ref):
    o_ref[0] = w_ref[...].astype(BF16)


def _pack_w_up(w_up):
    d = w_up.shape[0]
    n_f = D_FF // FFN_TF
    return pl.pallas_call(
        _pack_w_up_kernel,
        grid=(n_f, 2),
        in_specs=[pl.BlockSpec((d, FFN_TF), lambda f, h: (0, h * n_f + f))],
        out_specs=pl.BlockSpec((1, d, FFN_TF), lambda f, h: (f, 0, h)),
        out_shape=jax.ShapeDtypeStruct((n_f, d, 2 * FFN_TF), BF16),
        compiler_params=_params("parallel", "parallel"),
        name="pack_w_up",
    )(w_up)


def _in_proj_kernel(x_ref, g_ref, w_ref, wkr_ref, z_ref, kr_ref, h_ref, *, tm, n_blk):
    s = pl.program_id(1)
    n_pairs = n_blk // 2

    @pl.when(s == 0)
    def _():
        _rms_rows_to(x_ref, g_ref, h_ref, tm, 128)
        kr_ref[...] = lax.dot_general(h_ref[...], wkr_ref[...], _NT, preferred_element_type=F32)

    def block(b):
        z_ref[:, b * IN_TN:(b + 1) * IN_TN] = lax.dot_general(
            h_ref[...], w_ref[b], _NT, preferred_element_type=F32).astype(BF16)

    @pl.when(s < n_pairs)
    def _():
        block(0)
        block(1)

    if n_blk % 2:
        pl.when(s == n_pairs)(functools.partial(block, 0))


def _in_proj(x2, g, w_blk, w_kr, *, tm=1024):
    T = x2.shape[0]
    tn = IN_TN
    n_blk = w_blk.shape[0]
    return pl.pallas_call(
        functools.partial(_in_proj_kernel, tm=tm, n_blk=n_blk),
        grid=(T // tm, (n_blk + 1) // 2),
        in_specs=[
            pl.BlockSpec((tm, D_MODEL), lambda i, j: (i, 0)),
            pl.BlockSpec((1, D_MODEL), lambda i, j: (0, 0)),
            pl.BlockSpec((2, tn, D_MODEL), lambda i, j: (j, 0, 0)),
            pl.BlockSpec((LANES, D_MODEL), lambda i, j: (0, 0)),
        ],
        out_specs=[
            pl.BlockSpec((tm, 2 * tn), lambda i, j: (i, j)),
            pl.BlockSpec((tm, LANES), lambda i, j: (i, 0)),
        ],
        out_shape=[
            jax.ShapeDtypeStruct((T, Z_COLS), BF16),
            jax.ShapeDtypeStruct((T, LANES), F32),
        ],
        scratch_shapes=[pltpu.VMEM((tm, D_MODEL), BF16)],
        compiler_params=_params("parallel", "arbitrary"),
        name="in_proj",
    )(x2, g, w_blk, w_kr)


def _qkv_kernel(cq_ref, ckv_ref, kr_ref, pos_ref, invf_ref, qn_ref, kvn_ref, wuq_ref, wukv_ref,
                q_ref, k_ref, v_ref, *, tm):
    ang = pos_ref[...].astype(F32) * invf_ref[...]
    cos = jnp.cos(ang)
    sin = jnp.sin(ang)
    lane = lax.broadcasted_iota(jnp.int32, (tm, LANES), 1)
    half = QK_ROPE // 2
    c_tab = jnp.where(lane < QK_ROPE, cos, 0.0)
    s_lo = jnp.where(lane < half, -sin, 0.0)
    s_hi = jnp.where((lane >= half) & (lane < QK_ROPE), sin, 0.0)

    def rope(t, tabs):
        c, lo, hi = tabs
        return t * c + pltpu.roll(t, LANES - half, 1) * lo + pltpu.roll(t, half, 1) * hi

    k_tabs = (c_tab, s_lo, s_hi)
    q_tabs = tuple(t * Q_SCALE for t in k_tabs)
    cqn = _rms(cq_ref[...].astype(F32), qn_ref[...]).astype(BF16)
    q = jnp.dot(cqn, wuq_ref[...], preferred_element_type=F32)
    nope_cols = MLA_HEADS * QK_NOPE
    for h in range(MLA_HEADS):
        q_ref[:, h * HEAD_PAD:h * HEAD_PAD + QK_NOPE] = q[:, h * QK_NOPE:(h + 1) * QK_NOPE].astype(BF16)
        q_ref[:, h * HEAD_PAD + QK_NOPE:(h + 1) * HEAD_PAD] = (
            rope(q[:, nope_cols + h * LANES:nope_cols + (h + 1) * LANES], q_tabs)).astype(BF16)

    ckvn = _rms(ckv_ref[...].astype(F32), kvn_ref[...]).astype(BF16)
    kv = jnp.dot(ckvn, wukv_ref[...], preferred_element_type=F32)
    k_rope = rope(kr_ref[...], k_tabs).astype(BF16)
    ones = jnp.ones((tm, V_HEAD), BF16)
    for h in range(MLA_HEADS):
        k_ref[:, h * HEAD_PAD:h * HEAD_PAD + QK_NOPE] = kv[:, h * HEAD_PAD:h * HEAD_PAD + QK_NOPE].astype(BF16)
        k_ref[:, h * HEAD_PAD + QK_NOPE:(h + 1) * HEAD_PAD] = k_rope
        v_ref[:, 2 * h * V_HEAD:(2 * h + 1) * V_HEAD] = kv[:, h * HEAD_PAD + QK_NOPE:(h + 1) * HEAD_PAD].astype(BF16)
        v_ref[:, (2 * h + 1) * V_HEAD:(2 * h + 2) * V_HEAD] = ones


def _qkv(z, kr, pos, invf, q_norm, kv_norm, w_uq_p, w_ukv, *, tm=512):
    T = z.shape[0]
    qk_cols = MLA_HEADS * HEAD_PAD
    return pl.pallas_call(
        functools.partial(_qkv_kernel, tm=tm),
        grid=(T // tm,),
        in_specs=[
            pl.BlockSpec((tm, Q_LORA), lambda i: (i, Z_CQ // Q_LORA)),
            pl.BlockSpec((tm, KV_LORA), lambda i: (i, Z_CKV // KV_LORA)),
            pl.BlockSpec((tm, LANES), lambda i: (i, 0)),
            pl.BlockSpec((tm, 1), lambda i: (i, 0)),
            pl.BlockSpec((1, LANES), lambda i: (0, 0)),
            pl.BlockSpec((1, Q_LORA), lambda i: (0, 0)),
            pl.BlockSpec((1, KV_LORA), lambda i: (0, 0)),
            pl.BlockSpec((Q_LORA, qk_cols), lambda i: (0, 0)),
            pl.BlockSpec((KV_LORA, qk_cols), lambda i: (0, 0)),
        ],
        out_specs=[
            pl.BlockSpec((tm, qk_cols), lambda i: (i, 0)),
            pl.BlockSpec((tm, qk_cols), lambda i: (i, 0)),
            pl.BlockSpec((tm, 2 * MLA_HEADS * V_HEAD), lambda i: (i, 0)),
        ],
        out_shape=[
            jax.ShapeDtypeStruct((T, qk_cols), BF16),
            jax.ShapeDtypeStruct((T, qk_cols), BF16),
            jax.ShapeDtypeStruct((T, 2 * MLA_HEADS * V_HEAD), BF16),
        ],
        compiler_params=_params("parallel"),
        name="qkv",
    )(z, z, kr, pos, invf, q_norm, kv_norm, w_uq_p, w_ukv)


def _mla_kernel(q_ref, k_ref, v_ref, o_ref, *, seq, tq, heads):
    for h in range(heads):
        k = k_ref[:, h * HEAD_PAD:(h + 1) * HEAD_PAD]
        v = v_ref[:, 2 * h * V_HEAD:2 * (h + 1) * V_HEAD]
        for c in range(seq // tq):
            r = slice(c * tq, (c + 1) * tq)
            s = lax.dot_general(q_ref[r, h * HEAD_PAD:(h + 1) * HEAD_PAD], k, (((1,), (1,)), ((), ())),
                                preferred_element_type=F32)
            m = jnp.max(s, axis=-1, keepdims=True)
            p = jnp.exp2(s - m).astype(BF16)
            ov = jnp.dot(p, v, preferred_element_type=F32)
            o_ref[r, h * V_HEAD:(h + 1) * V_HEAD] = (ov[:, :V_HEAD] / ov[:, V_HEAD:]).astype(BF16)


def _mla(q, k, v, *, batch, seq, tq=512, heads=4):
    T = q.shape[0]
    return pl.pallas_call(
        functools.partial(_mla_kernel, seq=seq, tq=tq, heads=heads),
        grid=(batch, MLA_HEADS // heads),
        in_specs=[
            pl.BlockSpec((seq, heads * HEAD_PAD), lambda b, h: (b, h)),
            pl.BlockSpec((seq, heads * HEAD_PAD), lambda b, h: (b, h)),
            pl.BlockSpec((seq, heads * 2 * V_HEAD), lambda b, h: (b, h)),
        ],
        out_specs=pl.BlockSpec((seq, heads * V_HEAD), lambda b, h: (b, h)),
        out_shape=jax.ShapeDtypeStruct((T, MLA_HEADS * V_HEAD), BF16),
        compiler_params=_params("parallel", "parallel"),
        name="mla_attn",
    )(q, k, v)


def _mem_kv_kernel(m_ref, g_ref, w_ref, o_ref, h_ref, *, tm):
    @pl.when(pl.program_id(1) == 0)
    def _():
        _rms_rows_to(m_ref, g_ref, h_ref, tm, 128)

    o_ref[...] = jnp.dot(h_ref[...], w_ref[...], preferred_element_type=F32).astype(BF16)


def _mem_kv(mem2, g, w, *, tm=1024, tn=1024):
    R = mem2.shape[0]
    N = w.shape[1]
    return pl.pallas_call(
        functools.partial(_mem_kv_kernel, tm=tm),
        grid=(R // tm, N // tn),
        in_specs=[
            pl.BlockSpec((tm, D_MODEL), lambda i, j: (i, 0)),
            pl.BlockSpec((1, D_MODEL), lambda i, j: (0, 0)),
            pl.BlockSpec((D_MODEL, tn), lambda i, j: (0, j)),
        ],
        out_specs=pl.BlockSpec((tm, tn), lambda i, j: (i, j)),
        out_shape=jax.ShapeDtypeStruct((R, N), BF16),
        scratch_shapes=[pltpu.VMEM((tm, D_MODEL), BF16)],
        compiler_params=_params("parallel", "arbitrary"),
        name="mem_kv",
    )(mem2, g, w)


def _mem_attn_kernel(q_ref, kv_ref, o_ref):
    scale = X_HEAD_DIM ** -0.5
    for h in range(X_HEADS):
        c = slice(h * X_HEAD_DIM, (h + 1) * X_HEAD_DIM)
        k = kv_ref[:, h * X_HEAD_DIM:(h + 1) * X_HEAD_DIM]
        v = kv_ref[:, X_WIDTH + h * X_HEAD_DIM:X_WIDTH + (h + 1) * X_HEAD_DIM]
        s = lax.dot_general(q_ref[:, c], k, (((1,), (1,)), ((), ())), preferred_element_type=F32) * scale
        m = jnp.max(s, axis=-1, keepdims=True)
        p = jnp.exp(s - m)
        l = jnp.sum(p, axis=-1, keepdims=True)
        o = jnp.dot(p.astype(BF16), v, preferred_element_type=F32)
        o_ref[:, c] = (o / l).astype(BF16)


def _mem_attn(z, kvm, *, seq, mem_len, tm=1024):
    T = z.shape[0]
    return pl.pallas_call(
        _mem_attn_kernel,
        grid=(T // tm,),
        in_specs=[
            pl.BlockSpec((tm, X_WIDTH), lambda i: (i, Z_QX // X_WIDTH)),
            pl.BlockSpec((mem_len, 2 * X_WIDTH), lambda i: (i // (seq // tm), 0)),
        ],
        out_specs=pl.BlockSpec((tm, X_WIDTH), lambda i: (i, 0)),
        out_shape=jax.ShapeDtypeStruct((T, X_WIDTH), BF16),
        compiler_params=_params("parallel"),
        name="mem_attn",
    )(z, kvm)


def _merge_kernel(o_ref, cv_ref, cb_ref, cc_ref, cvp_ref, ccp_ref, cvn_ref, ccn_ref, yx_ref,
                  g0_ref, g1_ref, g2_ref, b_ref, cw_ref,
                  wa_ref, wc_ref, wm_ref, out_ref, pext_ref, *, tm, tn, tiles_per_seq):
    pos_in_seq = pl.program_id(0) % tiles_per_seq
    keep_prev = jnp.where(pos_in_seq == 0, 0.0, 1.0)
    keep_next = jnp.where(pos_in_seq == tiles_per_seq - 1, 0.0, 1.0)
    h8 = SUBLANES_F32
    prev = (cvp_ref[h8:, :].astype(F32) * ccp_ref[h8:, :].astype(F32)) * keep_prev
    nxt = (cvn_ref[:h8, :].astype(F32) * ccn_ref[:h8, :].astype(F32)) * keep_next
    pext_ref[0:h8, :] = prev
    pext_ref[tm + h8:tm + 2 * h8, :] = nxt
    pext_ref[h8:tm + h8, :] = cv_ref[...].astype(F32) * cc_ref[...].astype(F32)
    conv = (cw_ref[0:1, :] * pext_ref[h8 - 1:tm + h8 - 1, :] + cw_ref[1:2, :] * pext_ref[h8:tm + h8, :]
            + cw_ref[2:3, :] * pext_ref[h8 + 1:tm + h8 + 1, :])
    cbr = (cb_ref[...].astype(F32) * conv).astype(BF16)

    def gate(g_ref, k, c):
        return jax.nn.sigmoid(g_ref[:, c].astype(F32) + b_ref[:, k * D_MODEL + c.start:k * D_MODEL + c.stop])

    for n in range(D_MODEL // tn):
        c = slice(n * tn, (n + 1) * tn)
        y = gate(g0_ref, 0, c) * jnp.dot(o_ref[...], wa_ref[:, c], preferred_element_type=F32)
        y += gate(g1_ref, 1, c) * jnp.dot(cbr, wc_ref[:, c], preferred_element_type=F32)
        y += gate(g2_ref, 2, c) * jnp.dot(yx_ref[...], wm_ref[:, c], preferred_element_type=F32)
        out_ref[:, c] = y.astype(BF16)


def _merge(o_mla, z, yx, gate_bias, conv_w, w_o_mla, w_out_conv, w_o_mem, *, seq, tm=512, tn=2048):
    T = z.shape[0]
    hb = SUBLANES_BF16
    rb = tm // hb
    last_hb = T // hb - 1
    cw = CONV_WIDTH
    once = pl.Buffered(1)

    def zcol(c0, width):
        return c0 // width

    def gate_spec(k):
        return pl.BlockSpec((tm, D_MODEL), lambda i, k=k: (i, Z_G // D_MODEL + k))

    def prev_spec(c0):
        return pl.BlockSpec((hb, cw), lambda i: (jnp.maximum(i * rb - 1, 0), zcol(c0, cw)))

    def next_spec(c0):
        return pl.BlockSpec((hb, cw), lambda i: (jnp.minimum((i + 1) * rb, last_hb), zcol(c0, cw)))

    def w_spec():
        return pl.BlockSpec((cw, D_MODEL), lambda i: (0, 0), pipeline_mode=once)

    assert Z_G % D_MODEL == 0
    return pl.pallas_call(
        functools.partial(_merge_kernel, tm=tm, tn=tn, tiles_per_seq=seq // tm),
        grid=(T // tm,),
        in_specs=[
            pl.BlockSpec((tm, MLA_HEADS * V_HEAD), lambda i: (i, 0)),
            pl.BlockSpec((tm, cw), lambda i: (i, zcol(Z_CV, cw))),
            pl.BlockSpec((tm, cw), lambda i: (i, zcol(Z_CB, cw))),
            pl.BlockSpec((tm, cw), lambda i: (i, zcol(Z_CC, cw))),
            prev_spec(Z_CV), prev_spec(Z_CC), next_spec(Z_CV), next_spec(Z_CC),
            pl.BlockSpec((tm, X_WIDTH), lambda i: (i, 0)),
            gate_spec(0), gate_spec(1), gate_spec(2),
            pl.BlockSpec((1, N_BRANCH * D_MODEL), lambda i: (0, 0), pipeline_mode=once),
            pl.BlockSpec((3, cw), lambda i: (0, 0), pipeline_mode=once),
            w_spec(), w_spec(), w_spec(),
        ],
        out_specs=pl.BlockSpec((tm, D_MODEL), lambda i: (i, 0)),
        out_shape=jax.ShapeDtypeStruct((T, D_MODEL), BF16),
        scratch_shapes=[pltpu.VMEM((tm + 2 * SUBLANES_F32, cw), F32)],
        compiler_params=_params("parallel"),
        name="merge",
    )(o_mla, z, z, z, z, z, z, z, yx, z, z, z, gate_bias, conv_w, w_o_mla, w_out_conv, w_o_mem)


def _out_proj_kernel(m_ref, w_ref, x_ref, o_ref):
    o_ref[...] = x_ref[...] + jnp.dot(m_ref[...], w_ref[...], preferred_element_type=F32)


def _out_proj(merged, w_o, x2, *, tm=512):
    T = x2.shape[0]
    return pl.pallas_call(
        _out_proj_kernel,
        grid=(T // tm,),
        in_specs=[
            pl.BlockSpec((tm, D_MODEL), lambda i: (i, 0)),
            pl.BlockSpec((D_MODEL, D_MODEL), lambda i: (0, 0), pipeline_mode=pl.Buffered(1)),
            pl.BlockSpec((tm, D_MODEL), lambda i: (i, 0)),
        ],
        out_specs=pl.BlockSpec((tm, D_MODEL), lambda i: (i, 0)),
        out_shape=jax.ShapeDtypeStruct((T, D_MODEL), F32),
        compiler_params=_params("parallel"),
        name="out_proj",
    )(merged, w_o, x2)


FFN_COL_CHUNK = 256


def _ffn_kernel(x_ref, xp_ref, xn_ref, g_ref, wu_ref, ct_ref, wd_ref, fg_ref,
                o_ref, hn_ref, u_ref, *, tm, tf, tiles_per_seq, n_f, final):
    acc_ref = o_ref
    i = pl.program_id(0)
    s = pl.program_id(1)
    h8 = SUBLANES_F32
    cw = FFN_COL_CHUNK
    rows = 128
    n_chunks = tf // cw
    n_pairs = n_f // 2
    n_steps = n_pairs + n_f % 2

    @pl.when(s == 0)
    def _():
        pos_in_seq = i % tiles_per_seq
        keep_prev = jnp.where(pos_in_seq == 0, 0.0, 1.0)
        keep_next = jnp.where(pos_in_seq == tiles_per_seq - 1, 0.0, 1.0)
        halo = jnp.concatenate([_rms(xp_ref[...], g_ref[...]) * keep_prev,
                                _rms(xn_ref[...], g_ref[...]) * keep_next], axis=0)
        hn_ref[tm:tm + 2 * h8, :] = halo.astype(BF16)
        _rms_rows_to(x_ref, g_ref, hn_ref, tm, rows)
        acc_ref[...] = jnp.zeros_like(acc_ref)

    def up(fb, u_ref):
        r = jnp.dot(hn_ref[...], wu_ref[fb], preferred_element_type=F32)
        u_ref[h8:tm + h8, :] = r[:tm]
        u_ref[0:h8, :] = r[tm:tm + h8]
        u_ref[tm + h8:tm + 2 * h8, :] = r[tm + h8:]

    def conv(u_ref, fb, c0):
        return (ct_ref[fb, 0:1, c0:c0 + cw] * u_ref[h8 - 1:tm + h8 - 1, c0:c0 + cw]
                + ct_ref[fb, 1:2, c0:c0 + cw] * u_ref[h8:tm + h8, c0:c0 + cw]
                + ct_ref[fb, 2:3, c0:c0 + cw] * u_ref[h8 + 1:tm + h8 + 1, c0:c0 + cw])

    def block(fb):
        up(fb, u_ref.at[fb])
        down = None
        for n in range(n_chunks):
            c0 = n * cw
            a = conv(u_ref.at[fb], fb, c0)
            b = conv(u_ref.at[fb], fb, tf + c0)
            act = (a * jax.nn.sigmoid(a) * b).astype(BF16)
            d = jnp.dot(act, wd_ref[fb * tf + c0:fb * tf + c0 + cw, :], preferred_element_type=F32)
            down = d if down is None else down + d
        acc_ref[...] += down

    @pl.when(s < n_pairs)
    def _():
        block(0)
        block(1)

    if n_f % 2:
        pl.when(s == n_pairs)(functools.partial(block, 0))

    @pl.when(s == n_steps - 1)
    def _():
        def body(c, carry):
            r = pl.ds(pl.multiple_of(c * rows, rows), rows)
            y = x_ref[r, :] + acc_ref[r, :]
            o_ref[r, :] = _rms(y, fg_ref[...]) if final else y
            return carry
        lax.fori_loop(0, tm // rows, body, 0)


def _ffn(x1, ffn_norm, w_up_blk, conv_taps, w_down, final_norm, *, seq, final, tm=512):
    T = x1.shape[0]
    h8 = SUBLANES_F32
    cw = FFN_COL_CHUNK
    rb = tm // h8
    last_hb = T // h8 - 1
    n_f = w_up_blk.shape[0]
    tf = w_up_blk.shape[2] // 2
    n_steps = (n_f + 1) // 2
    return pl.pallas_call(
        functools.partial(_ffn_kernel, tm=tm, tf=tf, tiles_per_seq=seq // tm, n_f=n_f, final=final),
        grid=(T // tm, n_steps),
        in_specs=[
            pl.BlockSpec((tm, D_MODEL), lambda i, s: (i, 0)),
            pl.BlockSpec((h8, D_MODEL), lambda i, s: (jnp.maximum(i * rb - 1, 0), 0)),
            pl.BlockSpec((h8, D_MODEL), lambda i, s: (jnp.minimum((i + 1) * rb, last_hb), 0)),
            pl.BlockSpec((1, D_MODEL), lambda i, s: (0, 0)),
            pl.BlockSpec((2, D_MODEL, 2 * tf), lambda i, s: (s, 0, 0)),
            pl.BlockSpec((2, 3, 2 * tf), lambda i, s: (s, 0, 0)),
            pl.BlockSpec((2 * tf, D_MODEL), lambda i, s: (s, 0)),
            pl.BlockSpec((1, D_MODEL), lambda i, s: (0, 0)),
        ],
        out_specs=pl.BlockSpec((tm, D_MODEL), lambda i, s: (i, 0)),
        out_shape=jax.ShapeDtypeStruct((T, D_MODEL), F32),
        scratch_shapes=[
            pltpu.VMEM((tm + 2 * h8, D_MODEL), BF16),
            pltpu.VMEM((2, tm + 2 * h8, 2 * tf), F32),
        ],
        compiler_params=_params("parallel", "arbitrary"),
        name="ffn",
    )(x1, x1, x1, ffn_norm, w_up_blk, conv_taps, w_down, final_norm)


def _block_conv_taps(ffn_conv_w):
    k = ffn_conv_w.shape[0]
    n_f = D_FF // FFN_TF
    halves = ffn_conv_w.reshape(k, 2, n_f, FFN_TF)
    return jnp.transpose(halves, (2, 0, 1, 3)).reshape(n_f, k, 2 * FFN_TF)


def _pack_w_uq(w_uq):
    w = w_uq.reshape(Q_LORA, MLA_HEADS, QK_NOPE + QK_ROPE)
    nope = (w[:, :, :QK_NOPE] * Q_SCALE).reshape(Q_LORA, MLA_HEADS * QK_NOPE)
    rope = jnp.pad(w[:, :, QK_NOPE:], ((0, 0), (0, 0), (0, LANES - QK_ROPE))).reshape(Q_LORA, MLA_HEADS * LANES)
    return jnp.concatenate([nope, rope], axis=1).astype(BF16)


def kernel(x, mem, positions, mix_norm, w_in, q_norm, w_uq, kv_norm, w_ukv, w_o_mla, conv_w, w_out_conv,
           mem_norm, w_mem_kv, w_o_mem, gate_bias, w_o, ffn_norm, w_up, ffn_conv_w, w_down, final_norm):
    B, S, D = x.shape
    M = mem.shape[1]
    T = B * S
    depth = w_in.shape[0]
    x2 = x.reshape(T, D)
    mem2 = mem.reshape(B * M, D)
    pos = positions.reshape(T, 1)
    inv_freq = jnp.power(ROPE_THETA, -jnp.arange(0, QK_ROPE, 2, dtype=F32) / QK_ROPE)
    invf = jnp.concatenate([inv_freq, inv_freq, jnp.zeros((LANES - QK_ROPE,), F32)]).reshape(1, LANES)

    for l in range(depth):
        w_blk, w_kr = _pack_w_in(jnp.swapaxes(w_in[l], 0, 1))
        z, kr = _in_proj(x2, mix_norm[l].reshape(1, D), w_blk, w_kr)
        q, k, v = _qkv(z, kr, pos, invf, q_norm[l].reshape(1, -1), kv_norm[l].reshape(1, -1),
                       _pack_w_uq(w_uq[l]), w_ukv[l].astype(BF16))
        o_mla = _mla(q, k, v, batch=B, seq=S)
        kvm = _mem_kv(mem2, mem_norm[l].reshape(1, D), w_mem_kv[l].astype(BF16))
        yx = _mem_attn(z, kvm, seq=S, mem_len=M)
        merged = _merge(o_mla, z, yx, gate_bias[l].reshape(1, -1), conv_w[l],
                        w_o_mla[l].astype(BF16), w_out_conv[l].astype(BF16), w_o_mem[l].astype(BF16), seq=S)
        x2 = _out_proj(merged, w_o[l].astype(BF16), x2)
        x2 = _ffn(x2, ffn_norm[l].reshape(1, D), _pack_w_up(w_up[l]), _block_conv_taps(ffn_conv_w[l]),
                  w_down[l].astype(BF16), final_norm.reshape(1, D), seq=S, final=(l == depth - 1))
    return x2.reshape(B, S, D)
```

```python
import functools

import jax
import jax.numpy as jnp
from jax import lax
from jax.experimental import pallas as pl
from jax.experimental.pallas import tpu as pltpu

F32 = jnp.float32
BF16 = jnp.bfloat16

D_MODEL = 2048
MLA_HEADS = 8
Q_LORA = 512
KV_LORA = 512
QK_NOPE = 128
QK_ROPE = 64
V_HEAD = 128
ROPE_THETA = 10000.0
CONV_WIDTH = 1024
X_HEADS = 4
X_HEAD_DIM = 256
X_WIDTH = X_HEADS * X_HEAD_DIM
D_FF = 5632
N_BRANCH = 3
EPS = 1e-6
LOG2_E = 1.4426950408889634
Q_SCALE = (QK_NOPE + QK_ROPE) ** -0.5 * LOG2_E

LANES = 128
SUBLANES_F32 = 8
SUBLANES_BF16 = 16
HEAD_PAD = 256
VMEM_LIMIT = 56 * 1024 * 1024

Z_G = 0
Z_CQ = Z_G + N_BRANCH * D_MODEL
Z_CKV = Z_CQ + Q_LORA
Z_CV = Z_CKV + KV_LORA
Z_CB = Z_CV + CONV_WIDTH
Z_CC = Z_CB + CONV_WIDTH
Z_QX = Z_CC + CONV_WIDTH
Z_COLS = Z_QX + X_WIDTH


_NT = (((1,), (1,)), ((), ()))


def _params(*sem):
    return pltpu.CompilerParams(dimension_semantics=sem, vmem_limit_bytes=VMEM_LIMIT)


def _rms(x, g):
    inv = lax.rsqrt(jnp.mean(x * x, axis=-1, keepdims=True) + EPS)
    return x * inv * g


def _rms_rows_to(x_ref, g_ref, out_ref, rows, chunk, out_row0=0):
    def body(c, carry):
        r0 = pl.multiple_of(c * chunk, chunk)
        x = x_ref[pl.ds(r0, chunk), :].astype(F32)
        out_ref[pl.ds(out_row0 + r0, chunk), :] = _rms(x, g_ref[...]).astype(out_ref.dtype)
        return carry
    lax.fori_loop(0, rows // chunk, body, 0)


IN_TN = 1024
FFN_TF = 512


N_GATE_BLK = N_BRANCH * D_MODEL // IN_TN


def _pack_w_in_kernel(a_ref, b_ref, kr_ref, wblk_ref, wkr_ref):
    j = pl.program_id(0)

    @pl.when(j == N_GATE_BLK)
    def _():
        wblk_ref[0] = a_ref[...].astype(BF16)

    @pl.when(j != N_GATE_BLK)
    def _():
        wblk_ref[0, :IN_TN - QK_ROPE, :] = a_ref[QK_ROPE:, :].astype(BF16)
        wblk_ref[0, IN_TN - QK_ROPE:, :] = b_ref[...].astype(BF16)

    @pl.when(j == 0)
    def _():
        wkr_ref[:QK_ROPE, :] = kr_ref[...].astype(BF16)
        wkr_ref[QK_ROPE:, :] = jnp.zeros((LANES - QK_ROPE, wkr_ref.shape[1]), BF16)


def _pack_w_in(wi_t):
    cols, d = wi_t.shape
    kr0 = Q_LORA + KV_LORA
    assert cols == Z_COLS + QK_ROPE and kr0 == IN_TN and IN_TN % QK_ROPE == 0
    n_blk = Z_COLS // IN_TN
    n_tail = n_blk - 1 - N_GATE_BLK
    sub = IN_TN // QK_ROPE

    def src_blk(j):
        return jnp.where(j < N_GATE_BLK, j + 1 + n_tail, jnp.where(j == N_GATE_BLK, 0, j - N_GATE_BLK))

    return pl.pallas_call(
        _pack_w_in_kernel,
        grid=(n_blk,),
        in_specs=[
            pl.BlockSpec((IN_TN, d), lambda j: (src_blk(j), 0)),
            pl.BlockSpec((QK_ROPE, d), lambda j: ((src_blk(j) + 1) * sub, 0)),
            pl.BlockSpec((QK_ROPE, d), lambda j: (kr0 // QK_ROPE, 0)),
        ],
        out_specs=[
            pl.BlockSpec((1, IN_TN, d), lambda j: (j, 0, 0)),
            pl.BlockSpec((LANES, d), lambda j: (0, 0)),
        ],
        out_shape=[
            jax.ShapeDtypeStruct((n_blk, IN_TN, d), BF16),
            jax.ShapeDtypeStruct((LANES, d), BF16),
        ],
        compiler_params=_params("arbitrary"),
        name="pack_w_in",
    )(wi_t, wi_t, wi_t)


def _pack_w_up_kernel(w_ref, o_ref):
    o_ref[0] = w_ref[...].astype(BF16)


def _pack_w_up(w_up):
    d = w_up.shape[0]
    n_f = D_FF // FFN_TF
    return pl.pallas_call(
        _pack_w_up_kernel,
        grid=(n_f, 2),
        in_specs=[pl.BlockSpec((d, FFN_TF), lambda f, h: (0, h * n_f + f))],
        out_specs=pl.BlockSpec((1, d, FFN_TF), lambda f, h: (f, 0, h)),
        out_shape=jax.ShapeDtypeStruct((n_f, d, 2 * FFN_TF), BF16),
        compiler_params=_params("parallel", "parallel"),
        name="pack_w_up",
    )(w_up)


def _in_proj_kernel(x_ref, g_ref, w_ref, wkr_ref, z_ref, kr_ref, h_ref, *, tm):
    @pl.when(pl.program_id(1) == 0)
    def _():
        _rms_rows_to(x_ref, g_ref, h_ref, tm, 128)
        kr_ref[...] = lax.dot_general(h_ref[...], wkr_ref[...], _NT, preferred_element_type=F32)

    z_ref[...] = lax.dot_general(h_ref[...], w_ref[0], _NT, preferred_element_type=F32).astype(BF16)


def _in_proj(x2, g, w_blk, w_kr, *, tm=1024):
    T = x2.shape[0]
    tn = IN_TN
    return pl.pallas_call(
        functools.partial(_in_proj_kernel, tm=tm),
        grid=(T // tm, Z_COLS // tn),
        in_specs=[
            pl.BlockSpec((tm, D_MODEL), lambda i, j: (i, 0)),
            pl.BlockSpec((1, D_MODEL), lambda i, j: (0, 0)),
            pl.BlockSpec((1, tn, D_MODEL), lambda i, j: (j, 0, 0)),
            pl.BlockSpec((LANES, D_MODEL), lambda i, j: (0, 0)),
        ],
        out_specs=[
            pl.BlockSpec((tm, tn), lambda i, j: (i, j)),
            pl.BlockSpec((tm, LANES), lambda i, j: (i, 0)),
        ],
        out_shape=[
            jax.ShapeDtypeStruct((T, Z_COLS), BF16),
            jax.ShapeDtypeStruct((T, LANES), F32),
        ],
        scratch_shapes=[pltpu.VMEM((tm, D_MODEL), BF16)],
        compiler_params=_params("parallel", "arbitrary"),
        name="in_proj",
    )(x2, g, w_blk, w_kr)


def _qkv_kernel(cq_ref, ckv_ref, kr_ref, pos_ref, invf_ref, qn_ref, kvn_ref, wuq_ref, wukv_ref,
                q_ref, k_ref, v_ref, *, tm):
    ang = pos_ref[...].astype(F32) * invf_ref[...]
    cos = jnp.cos(ang)
    sin = jnp.sin(ang)
    lane = lax.broadcasted_iota(jnp.int32, (tm, LANES), 1)
    half = QK_ROPE // 2
    c_tab = jnp.where(lane < QK_ROPE, cos, 0.0)
    s_lo = jnp.where(lane < half, -sin, 0.0)
    s_hi = jnp.where((lane >= half) & (lane < QK_ROPE), sin, 0.0)

    def rope(t, tabs):
        c, lo, hi = tabs
        return t * c + pltpu.roll(t, LANES - half, 1) * lo + pltpu.roll(t, half, 1) * hi

    k_tabs = (c_tab, s_lo, s_hi)
    q_tabs = tuple(t * Q_SCALE for t in k_tabs)
    cqn = _rms(cq_ref[...].astype(F32), qn_ref[...]).astype(BF16)
    q = jnp.dot(cqn, wuq_ref[...], preferred_element_type=F32)
    nope_cols = MLA_HEADS * QK_NOPE
    for h in range(MLA_HEADS):
        q_ref[:, h * HEAD_PAD:h * HEAD_PAD + QK_NOPE] = q[:, h * QK_NOPE:(h + 1) * QK_NOPE].astype(BF16)
        q_ref[:, h * HEAD_PAD + QK_NOPE:(h + 1) * HEAD_PAD] = (
            rope(q[:, nope_cols + h * LANES:nope_cols + (h + 1) * LANES], q_tabs)).astype(BF16)

    ckvn = _rms(ckv_ref[...].astype(F32), kvn_ref[...]).astype(BF16)
    kv = jnp.dot(ckvn, wukv_ref[...], preferred_element_type=F32)
    k_rope = rope(kr_ref[...], k_tabs).astype(BF16)
    ones = jnp.ones((tm, V_HEAD), BF16)
    for h in range(MLA_HEADS):
        k_ref[:, h * HEAD_PAD:h * HEAD_PAD + QK_NOPE] = kv[:, h * HEAD_PAD:h * HEAD_PAD + QK_NOPE].astype(BF16)
        k_ref[:, h * HEAD_PAD + QK_NOPE:(h + 1) * HEAD_PAD] = k_rope
        v_ref[:, 2 * h * V_HEAD:(2 * h + 1) * V_HEAD] = kv[:, h * HEAD_PAD + QK_NOPE:(h + 1) * HEAD_PAD].astype(BF16)
        v_ref[:, (2 * h + 1) * V_HEAD:(2 * h + 2) * V_HEAD] = ones


def _qkv(z, kr, pos, invf, q_norm, kv_norm, w_uq_p, w_ukv, *, tm=512):
    T = z.shape[0]
    qk_cols = MLA_HEADS * HEAD_PAD
    return pl.pallas_call(
        functools.partial(_qkv_kernel, tm=tm),
        grid=(T // tm,),
        in_specs=[
            pl.BlockSpec((tm, Q_LORA), lambda i: (i, Z_CQ // Q_LORA)),
            pl.BlockSpec((tm, KV_LORA), lambda i: (i, Z_CKV // KV_LORA)),
            pl.BlockSpec((tm, LANES), lambda i: (i, 0)),
            pl.BlockSpec((tm, 1), lambda i: (i, 0)),
            pl.BlockSpec((1, LANES), lambda i: (0, 0)),
            pl.BlockSpec((1, Q_LORA), lambda i: (0, 0)),
            pl.BlockSpec((1, KV_LORA), lambda i: (0, 0)),
            pl.BlockSpec((Q_LORA, qk_cols), lambda i: (0, 0)),
            pl.BlockSpec((KV_LORA, qk_cols), lambda i: (0, 0)),
        ],
        out_specs=[
            pl.BlockSpec((tm, qk_cols), lambda i: (i, 0)),
            pl.BlockSpec((tm, qk_cols), lambda i: (i, 0)),
            pl.BlockSpec((tm, 2 * MLA_HEADS * V_HEAD), lambda i: (i, 0)),
        ],
        out_shape=[
            jax.ShapeDtypeStruct((T, qk_cols), BF16),
            jax.ShapeDtypeStruct((T, qk_cols), BF16),
            jax.ShapeDtypeStruct((T, 2 * MLA_HEADS * V_HEAD), BF16),
        ],
        compiler_params=_params("parallel"),
        name="qkv",
    )(z, z, kr, pos, invf, q_norm, kv_norm, w_uq_p, w_ukv)


def _mla_kernel(q_ref, k_ref, v_ref, o_ref, *, seq, tq, heads):
    for h in range(heads):
        k = k_ref[:, h * HEAD_PAD:(h + 1) * HEAD_PAD]
        v = v_ref[:, 2 * h * V_HEAD:2 * (h + 1) * V_HEAD]
        for c in range(seq // tq):
            r = slice(c * tq, (c + 1) * tq)
            s = lax.dot_general(q_ref[r, h * HEAD_PAD:(h + 1) * HEAD_PAD], k, (((1,), (1,)), ((), ())),
                                preferred_element_type=F32)
            m = jnp.max(s, axis=-1, keepdims=True)
            p = jnp.exp2(s - m).astype(BF16)
            ov = jnp.dot(p, v, preferred_element_type=F32)
            o_ref[r, h * V_HEAD:(h + 1) * V_HEAD] = (ov[:, :V_HEAD] / ov[:, V_HEAD:]).astype(BF16)


def _mla(q, k, v, *, batch, seq, tq=512, heads=4):
    T = q.shape[0]
    return pl.pallas_call(
        functools.partial(_mla_kernel, seq=seq, tq=tq, heads=heads),
        grid=(batch, MLA_HEADS // heads),
        in_specs=[
            pl.BlockSpec((seq, heads * HEAD_PAD), lambda b, h: (b, h)),
            pl.BlockSpec((seq, heads * HEAD_PAD), lambda b, h: (b, h)),
            pl.BlockSpec((seq, heads * 2 * V_HEAD), lambda b, h: (b, h)),
        ],
        out_specs=pl.BlockSpec((seq, heads * V_HEAD), lambda b, h: (b, h)),
        out_shape=jax.ShapeDtypeStruct((T, MLA_HEADS * V_HEAD), BF16),
        compiler_params=_params("parallel", "parallel"),
        name="mla_attn",
    )(q, k, v)


def _mem_kv_kernel(m_ref, g_ref, w_ref, o_ref, h_ref, *, tm):
    @pl.when(pl.program_id(1) == 0)
    def _():
        _rms_rows_to(m_ref, g_ref, h_ref, tm, 128)

    o_ref[...] = jnp.dot(h_ref[...], w_ref[...], preferred_element_type=F32).astype(BF16)


def _mem_kv(mem2, g, w, *, tm=1024, tn=1024):
    R = mem2.shape[0]
    N = w.shape[1]
    return pl.pallas_call(
        functools.partial(_mem_kv_kernel, tm=tm),
        grid=(R // tm, N // tn),
        in_specs=[
            pl.BlockSpec((tm, D_MODEL), lambda i, j: (i, 0)),
            pl.BlockSpec((1, D_MODEL), lambda i, j: (0, 0)),
            pl.BlockSpec((D_MODEL, tn), lambda i, j: (0, j)),
        ],
        out_specs=pl.BlockSpec((tm, tn), lambda i, j: (i, j)),
        out_shape=jax.ShapeDtypeStruct((R, N), BF16),
        scratch_shapes=[pltpu.VMEM((tm, D_MODEL), BF16)],
        compiler_params=_params("parallel", "arbitrary"),
        name="mem_kv",
    )(mem2, g, w)


def _mem_attn_kernel(q_ref, kv_ref, o_ref):
    scale = X_HEAD_DIM ** -0.5
    for h in range(X_HEADS):
        c = slice(h * X_HEAD_DIM, (h + 1) * X_HEAD_DIM)
        k = kv_ref[:, h * X_HEAD_DIM:(h + 1) * X_HEAD_DIM]
        v = kv_ref[:, X_WIDTH + h * X_HEAD_DIM:X_WIDTH + (h + 1) * X_HEAD_DIM]
        s = lax.dot_general(q_ref[:, c], k, (((1,), (1,)), ((), ())), preferred_element_type=F32) * scale
        m = jnp.max(s, axis=-1, keepdims=True)
        p = jnp.exp(s - m)
        l = jnp.sum(p, axis=-1, keepdims=True)
        o = jnp.dot(p.astype(BF16), v, preferred_element_type=F32)
        o_ref[:, c] = (o / l).astype(BF16)


def _mem_attn(z, kvm, *, seq, mem_len, tm=1024):
    T = z.shape[0]
    return pl.pallas_call(
        _mem_attn_kernel,
        grid=(T // tm,),
        in_specs=[
            pl.BlockSpec((tm, X_WIDTH), lambda i: (i, Z_QX // X_WIDTH)),
            pl.BlockSpec((mem_len, 2 * X_WIDTH), lambda i: (i // (seq // tm), 0)),
        ],
        out_specs=pl.BlockSpec((tm, X_WIDTH), lambda i: (i, 0)),
        out_shape=jax.ShapeDtypeStruct((T, X_WIDTH), BF16),
        compiler_params=_params("parallel"),
        name="mem_attn",
    )(z, kvm)


def _merge_kernel(o_ref, cv_ref, cb_ref, cc_ref, cvp_ref, ccp_ref, cvn_ref, ccn_ref, yx_ref,
                  g0_ref, g1_ref, g2_ref, b_ref, cw_ref,
                  wa_ref, wc_ref, wm_ref, out_ref, pext_ref, *, tm, tn, tiles_per_seq):
    pos_in_seq = pl.program_id(0) % tiles_per_seq
    keep_prev = jnp.where(pos_in_seq == 0, 0.0, 1.0)
    keep_next = jnp.where(pos_in_seq == tiles_per_seq - 1, 0.0, 1.0)
    h8 = SUBLANES_F32
    prev = (cvp_ref[h8:, :].astype(F32) * ccp_ref[h8:, :].astype(F32)) * keep_prev
    nxt = (cvn_ref[:h8, :].astype(F32) * ccn_ref[:h8, :].astype(F32)) * keep_next
    pext_ref[0:h8, :] = prev
    pext_ref[tm + h8:tm + 2 * h8, :] = nxt
    pext_ref[h8:tm + h8, :] = cv_ref[...].astype(F32) * cc_ref[...].astype(F32)
    conv = (cw_ref[0:1, :] * pext_ref[h8 - 1:tm + h8 - 1, :] + cw_ref[1:2, :] * pext_ref[h8:tm + h8, :]
            + cw_ref[2:3, :] * pext_ref[h8 + 1:tm + h8 + 1, :])
    cbr = (cb_ref[...].astype(F32) * conv).astype(BF16)

    def gate(g_ref, k, c):
        return jax.nn.sigmoid(g_ref[:, c].astype(F32) + b_ref[:, k * D_MODEL + c.start:k * D_MODEL + c.stop])

    for n in range(D_MODEL // tn):
        c = slice(n * tn, (n + 1) * tn)
        y = gate(g0_ref, 0, c) * jnp.dot(o_ref[...], wa_ref[:, c], preferred_element_type=F32)
        y += gate(g1_ref, 1, c) * jnp.dot(cbr, wc_ref[:, c], preferred_element_type=F32)
        y += gate(g2_ref, 2, c) * jnp.dot(yx_ref[...], wm_ref[:, c], preferred_element_type=F32)
        out_ref[:, c] = y.astype(BF16)


def _merge(o_mla, z, yx, gate_bias, conv_w, w_o_mla, w_out_conv, w_o_mem, *, seq, tm=512, tn=2048):
    T = z.shape[0]
    hb = SUBLANES_BF16
    rb = tm // hb
    last_hb = T // hb - 1
    cw = CONV_WIDTH
    once = pl.Buffered(1)

    def zcol(c0, width):
        return c0 // width

    def gate_spec(k):
        return pl.BlockSpec((tm, D_MODEL), lambda i, k=k: (i, Z_G // D_MODEL + k))

    def prev_spec(c0):
        return pl.BlockSpec((hb, cw), lambda i: (jnp.maximum(i * rb - 1, 0), zcol(c0, cw)))

    def next_spec(c0):
        return pl.BlockSpec((hb, cw), lambda i: (jnp.minimum((i + 1) * rb, last_hb), zcol(c0, cw)))

    def w_spec():
        return pl.BlockSpec((cw, D_MODEL), lambda i: (0, 0), pipeline_mode=once)

    assert Z_G % D_MODEL == 0
    return pl.pallas_call(
        functools.partial(_merge_kernel, tm=tm, tn=tn, tiles_per_seq=seq // tm),
        grid=(T // tm,),
        in_specs=[
            pl.BlockSpec((tm, MLA_HEADS * V_HEAD), lambda i: (i, 0)),
            pl.BlockSpec((tm, cw), lambda i: (i, zcol(Z_CV, cw))),
            pl.BlockSpec((tm, cw), lambda i: (i, zcol(Z_CB, cw))),
            pl.BlockSpec((tm, cw), lambda i: (i, zcol(Z_CC, cw))),
            prev_spec(Z_CV), prev_spec(Z_CC), next_spec(Z_CV), next_spec(Z_CC),
            pl.BlockSpec((tm, X_WIDTH), lambda i: (i, 0)),
            gate_spec(0), gate_spec(1), gate_spec(2),
            pl.BlockSpec((1, N_BRANCH * D_MODEL), lambda i: (0, 0), pipeline_mode=once),
            pl.BlockSpec((3, cw), lambda i: (0, 0), pipeline_mode=once),
            w_spec(), w_spec(), w_spec(),
        ],
        out_specs=pl.BlockSpec((tm, D_MODEL), lambda i: (i, 0)),
        out_shape=jax.ShapeDtypeStruct((T, D_MODEL), BF16),
        scratch_shapes=[pltpu.VMEM((tm + 2 * SUBLANES_F32, cw), F32)],
        compiler_params=_params("parallel"),
        name="merge",
    )(o_mla, z, z, z, z, z, z, z, yx, z, z, z, gate_bias, conv_w, w_o_mla, w_out_conv, w_o_mem)


def _out_proj_kernel(m_ref, w_ref, x_ref, o_ref):
    o_ref[...] = x_ref[...] + jnp.dot(m_ref[...], w_ref[...], preferred_element_type=F32)


def _out_proj(merged, w_o, x2, *, tm=512):
    T = x2.shape[0]
    return pl.pallas_call(
        _out_proj_kernel,
        grid=(T // tm,),
        in_specs=[
            pl.BlockSpec((tm, D_MODEL), lambda i: (i, 0)),
            pl.BlockSpec((D_MODEL, D_MODEL), lambda i: (0, 0), pipeline_mode=pl.Buffered(1)),
            pl.BlockSpec((tm, D_MODEL), lambda i: (i, 0)),
        ],
        out_specs=pl.BlockSpec((tm, D_MODEL), lambda i: (i, 0)),
        out_shape=jax.ShapeDtypeStruct((T, D_MODEL), F32),
        compiler_params=_params("parallel"),
        name="out_proj",
    )(merged, w_o, x2)


FFN_COL_CHUNK = 256


def _ffn_kernel(x_ref, xp_ref, xn_ref, g_ref, wu_ref, ct_ref, wd_ref, fg_ref,
                o_ref, hn_ref, u_ref, *, tm, tf, tiles_per_seq, n_f, bps, final):
    acc_ref = o_ref
    i = pl.program_id(0)
    s = pl.program_id(1)
    h8 = SUBLANES_F32
    cw = FFN_COL_CHUNK
    rows = 128
    n_chunks = tf // cw
    n_full = n_f // bps
    n_steps = pl.cdiv(n_f, bps)

    @pl.when(s == 0)
    def _():
        pos_in_seq = i % tiles_per_seq
        keep_prev = jnp.where(pos_in_seq == 0, 0.0, 1.0)
        keep_next = jnp.where(pos_in_seq == tiles_per_seq - 1, 0.0, 1.0)
        halo = jnp.concatenate([_rms(xp_ref[...], g_ref[...]) * keep_prev,
                                _rms(xn_ref[...], g_ref[...]) * keep_next], axis=0)
        hn_ref[tm:tm + 2 * h8, :] = halo.astype(BF16)
        _rms_rows_to(x_ref, g_ref, hn_ref, tm, rows)
        acc_ref[...] = jnp.zeros_like(acc_ref)

    def up(fb, u_ref):
        r = jnp.dot(hn_ref[...], wu_ref[fb], preferred_element_type=F32)
        u_ref[h8:tm + h8, :] = r[:tm]
        u_ref[0:h8, :] = r[tm:tm + h8]
        u_ref[tm + h8:tm + 2 * h8, :] = r[tm + h8:]

    def conv(u_ref, fb, c0):
        return (ct_ref[fb, 0:1, c0:c0 + cw] * u_ref[h8 - 1:tm + h8 - 1, c0:c0 + cw]
                + ct_ref[fb, 1:2, c0:c0 + cw] * u_ref[h8:tm + h8, c0:c0 + cw]
                + ct_ref[fb, 2:3, c0:c0 + cw] * u_ref[h8 + 1:tm + h8 + 1, c0:c0 + cw])

    def block(fb):
        up(fb, u_ref.at[fb])
        down = None
        for n in range(n_chunks):
            c0 = n * cw
            a = conv(u_ref.at[fb], fb, c0)
            b = conv(u_ref.at[fb], fb, tf + c0)
            act = (a * jax.nn.sigmoid(a) * b).astype(BF16)
            d = jnp.dot(act, wd_ref[fb * tf + c0:fb * tf + c0 + cw, :], preferred_element_type=F32)
            down = d if down is None else down + d
        acc_ref[...] += down

    def blocks(count):
        for fb in range(count):
            block(fb)

    pl.when(s < n_full)(functools.partial(blocks, bps))
    if n_f % bps:
        pl.when(s == n_full)(functools.partial(blocks, n_f % bps))

    @pl.when(s == n_steps - 1)
    def _():
        def body(c, carry):
            r = pl.ds(pl.multiple_of(c * rows, rows), rows)
            y = x_ref[r, :] + acc_ref[r, :]
            o_ref[r, :] = _rms(y, fg_ref[...]) if final else y
            return carry
        lax.fori_loop(0, tm // rows, body, 0)


def _ffn(x1, ffn_norm, w_up_blk, conv_taps, w_down, final_norm, *, seq, final, tm=512, bps=1):
    T = x1.shape[0]
    h8 = SUBLANES_F32
    rb = tm // h8
    last_hb = T // h8 - 1
    n_f = w_up_blk.shape[0]
    tf = w_up_blk.shape[2] // 2
    return pl.pallas_call(
        functools.partial(_ffn_kernel, tm=tm, tf=tf, tiles_per_seq=seq // tm, n_f=n_f, bps=bps, final=final),
        grid=(T // tm, pl.cdiv(n_f, bps)),
        in_specs=[
            pl.BlockSpec((tm, D_MODEL), lambda i, s: (i, 0)),
            pl.BlockSpec((h8, D_MODEL), lambda i, s: (jnp.maximum(i * rb - 1, 0), 0)),
            pl.BlockSpec((h8, D_MODEL), lambda i, s: (jnp.minimum((i + 1) * rb, last_hb), 0)),
            pl.BlockSpec((1, D_MODEL), lambda i, s: (0, 0)),
            pl.BlockSpec((bps, D_MODEL, 2 * tf), lambda i, s: (s, 0, 0)),
            pl.BlockSpec((bps, 3, 2 * tf), lambda i, s: (s, 0, 0)),
            pl.BlockSpec((bps * tf, D_MODEL), lambda i, s: (s, 0)),
            pl.BlockSpec((1, D_MODEL), lambda i, s: (0, 0)),
        ],
        out_specs=pl.BlockSpec((tm, D_MODEL), lambda i, s: (i, 0)),
        out_shape=jax.ShapeDtypeStruct((T, D_MODEL), F32),
        scratch_shapes=[
            pltpu.VMEM((tm + 2 * h8, D_MODEL), BF16),
            pltpu.VMEM((bps, tm + 2 * h8, 2 * tf), F32),
        ],
        compiler_params=_params("parallel", "arbitrary"),
        name="ffn",
    )(x1, x1, x1, ffn_norm, w_up_blk, conv_taps, w_down, final_norm)


def _block_conv_taps(ffn_conv_w):
    k = ffn_conv_w.shape[0]
    n_f = D_FF // FFN_TF
    halves = ffn_conv_w.reshape(k, 2, n_f, FFN_TF)
    return jnp.transpose(halves, (2, 0, 1, 3)).reshape(n_f, k, 2 * FFN_TF)


def _pack_w_uq(w_uq):
    w = w_uq.reshape(Q_LORA, MLA_HEADS, QK_NOPE + QK_ROPE)
    nope = (w[:, :, :QK_NOPE] * Q_SCALE).reshape(Q_LORA, MLA_HEADS * QK_NOPE)
    rope = jnp.pad(w[:, :, QK_NOPE:], ((0, 0), (0, 0), (0, LANES - QK_ROPE))).reshape(Q_LORA, MLA_HEADS * LANES)
    return jnp.concatenate([nope, rope], axis=1).astype(BF16)


def kernel(x, mem, positions, mix_norm, w_in, q_norm, w_uq, kv_norm, w_ukv, w_o_mla, conv_w, w_out_conv,
           mem_norm, w_mem_kv, w_o_mem, gate_bias, w_o, ffn_norm, w_up, ffn_conv_w, w_down, final_norm):
    B, S, D = x.shape
    M = mem.shape[1]
    T = B * S
    depth = w_in.shape[0]
    x2 = x.reshape(T, D)
    mem2 = mem.reshape(B * M, D)
    pos = positions.reshape(T, 1)
    inv_freq = jnp.power(ROPE_THETA, -jnp.arange(0, QK_ROPE, 2, dtype=F32) / QK_ROPE)
    invf = jnp.concatenate([inv_freq, inv_freq, jnp.zeros((LANES - QK_ROPE,), F32)]).reshape(1, LANES)

    for l in range(depth):
        w_blk, w_kr = _pack_w_in(jnp.swapaxes(w_in[l], 0, 1))
        z, kr = _in_proj(x2, mix_norm[l].reshape(1, D), w_blk, w_kr)
        q, k, v = _qkv(z, kr, pos, invf, q_norm[l].reshape(1, -1), kv_norm[l].reshape(1, -1),
                       _pack_w_uq(w_uq[l]), w_ukv[l].astype(BF16))
        o_mla = _mla(q, k, v, batch=B, seq=S)
        kvm = _mem_kv(mem2, mem_norm[l].reshape(1, D), w_mem_kv[l].astype(BF16))
        yx = _mem_attn(z, kvm, seq=S, mem_len=M)
        merged = _merge(o_mla, z, yx, gate_bias[l].reshape(1, -1), conv_w[l],
                        w_o_mla[l].astype(BF16), w_out_conv[l].astype(BF16), w_o_mem[l].astype(BF16), seq=S)
        x2 = _out_proj(merged, w_o[l].astype(BF16), x2)
        x2 = _ffn(x2, ffn_norm[l].reshape(1, D), _pack_w_up(w_up[l]), _block_conv_taps(ffn_conv_w[l]),
                  w_down[l].astype(BF16), final_norm.reshape(1, D), seq=S, final=(l == depth - 1))
    return x2.reshape(B, S, D)
```

```python
import functools

import jax
import jax.numpy as jnp
from jax import lax
from jax.experimental import pallas as pl
from jax.experimental.pallas import tpu as pltpu

F32 = jnp.float32
BF16 = jnp.bfloat16

D_MODEL = 2048
MLA_HEADS = 8
Q_LORA = 512
KV_LORA = 512
QK_NOPE = 128
QK_ROPE = 64
V_HEAD = 128
ROPE_THETA = 10000.0
CONV_WIDTH = 1024
X_HEADS = 4
X_HEAD_DIM = 256
X_WIDTH = X_HEADS * X_HEAD_DIM
D_FF = 5632
N_BRANCH = 3
EPS = 1e-6
LOG2_E = 1.4426950408889634
Q_SCALE = (QK_NOPE + QK_ROPE) ** -0.5 * LOG2_E

LANES = 128
SUBLANES_F32 = 8
SUBLANES_BF16 = 16
HEAD_PAD = 256
VMEM_LIMIT = 56 * 1024 * 1024

Z_G = 0
Z_CQ = Z_G + N_BRANCH * D_MODEL
Z_CKV = Z_CQ + Q_LORA
Z_CV = Z_CKV + KV_LORA
Z_CB = Z_CV + CONV_WIDTH
Z_CC = Z_CB + CONV_WIDTH
Z_QX = Z_CC + CONV_WIDTH
Z_COLS = Z_QX + X_WIDTH


_NT = (((1,), (1,)), ((), ()))


def _params(*sem):
    return pltpu.CompilerParams(dimension_semantics=sem, vmem_limit_bytes=VMEM_LIMIT)


def _rms(x, g):
    inv = lax.rsqrt(jnp.mean(x * x, axis=-1, keepdims=True) + EPS)
    return x * inv * g


def _rms_rows_to(x_ref, g_ref, out_ref, rows, chunk, out_row0=0):
    def body(c, carry):
        r0 = pl.multiple_of(c * chunk, chunk)
        x = x_ref[pl.ds(r0, chunk), :].astype(F32)
        out_ref[pl.ds(out_row0 + r0, chunk), :] = _rms(x, g_ref[...]).astype(out_ref.dtype)
        return carry
    lax.fori_loop(0, rows // chunk, body, 0)


IN_TN = 1024
FFN_TF = 512


N_GATE_BLK = N_BRANCH * D_MODEL // IN_TN


def _pack_w_in_kernel(a_ref, b_ref, kr_ref, wblk_ref, wkr_ref):
    j = pl.program_id(0)

    @pl.when(j == N_GATE_BLK)
    def _():
        wblk_ref[0] = a_ref[...].astype(BF16)

    @pl.when(j != N_GATE_BLK)
    def _():
        wblk_ref[0, :IN_TN - QK_ROPE, :] = a_ref[QK_ROPE:, :].astype(BF16)
        wblk_ref[0, IN_TN - QK_ROPE:, :] = b_ref[...].astype(BF16)

    @pl.when(j == 0)
    def _():
        wkr_ref[:QK_ROPE, :] = kr_ref[...].astype(BF16)
        wkr_ref[QK_ROPE:, :] = jnp.zeros((LANES - QK_ROPE, wkr_ref.shape[1]), BF16)


def _pack_w_in(wi_t):
    cols, d = wi_t.shape
    kr0 = Q_LORA + KV_LORA
    assert cols == Z_COLS + QK_ROPE and kr0 == IN_TN and IN_TN % QK_ROPE == 0
    n_blk = Z_COLS // IN_TN
    n_tail = n_blk - 1 - N_GATE_BLK
    sub = IN_TN // QK_ROPE

    def src_blk(j):
        return jnp.where(j < N_GATE_BLK, j + 1 + n_tail, jnp.where(j == N_GATE_BLK, 0, j - N_GATE_BLK))

    return pl.pallas_call(
        _pack_w_in_kernel,
        grid=(n_blk,),
        in_specs=[
            pl.BlockSpec((IN_TN, d), lambda j: (src_blk(j), 0)),
            pl.BlockSpec((QK_ROPE, d), lambda j: ((src_blk(j) + 1) * sub, 0)),
            pl.BlockSpec((QK_ROPE, d), lambda j: (kr0 // QK_ROPE, 0)),
        ],
        out_specs=[
            pl.BlockSpec((1, IN_TN, d), lambda j: (j, 0, 0)),
            pl.BlockSpec((LANES, d), lambda j: (0, 0)),
        ],
        out_shape=[
            jax.ShapeDtypeStruct((n_blk, IN_TN, d), BF16),
            jax.ShapeDtypeStruct((LANES, d), BF16),
        ],
        compiler_params=_params("arbitrary"),
        name="pack_w_in",
    )(wi_t, wi_t, wi_t)


def _pack_w_up_kernel(w_ref, o_ref):
    o_ref[0] = w_ref[...].astype(BF16)


def _pack_w_up(w_up):
    d = w_up.shape[0]
    n_f = D_FF // FFN_TF
    return pl.pallas_call(
        _pack_w_up_kernel,
        grid=(n_f, 2),
        in_specs=[pl.BlockSpec((d, FFN_TF), lambda f, h: (0, h * n_f + f))],
        out_specs=pl.BlockSpec((1, d, FFN_TF), lambda f, h: (f, 0, h)),
        out_shape=jax.ShapeDtypeStruct((n_f, d, 2 * FFN_TF), BF16),
        compiler_params=_params("parallel", "parallel"),
        name="pack_w_up",
    )(w_up)


def _in_proj_kernel(x_ref, g_ref, w_ref, wkr_ref, z_ref, kr_ref, h_ref, *, tm):
    @pl.when(pl.program_id(1) == 0)
    def _():
        _rms_rows_to(x_ref, g_ref, h_ref, tm, 128)
        kr_ref[...] = lax.dot_general(h_ref[...], wkr_ref[...], _NT, preferred_element_type=F32)

    z_ref[...] = lax.dot_general(h_ref[...], w_ref[0], _NT, preferred_element_type=F32).astype(BF16)


def _in_proj(x2, g, w_blk, w_kr, *, tm=1024):
    T = x2.shape[0]
    tn = IN_TN
    return pl.pallas_call(
        functools.partial(_in_proj_kernel, tm=tm),
        grid=(T // tm, Z_COLS // tn),
        in_specs=[
            pl.BlockSpec((tm, D_MODEL), lambda i, j: (i, 0)),
            pl.BlockSpec((1, D_MODEL), lambda i, j: (0, 0)),
            pl.BlockSpec((1, tn, D_MODEL), lambda i, j: (j, 0, 0)),
            pl.BlockSpec((LANES, D_MODEL), lambda i, j: (0, 0)),
        ],
        out_specs=[
            pl.BlockSpec((tm, tn), lambda i, j: (i, j)),
            pl.BlockSpec((tm, LANES), lambda i, j: (i, 0)),
        ],
        out_shape=[
            jax.ShapeDtypeStruct((T, Z_COLS), BF16),
            jax.ShapeDtypeStruct((T, LANES), F32),
        ],
        scratch_shapes=[pltpu.VMEM((tm, D_MODEL), BF16)],
        compiler_params=_params("parallel", "arbitrary"),
        name="in_proj",
    )(x2, g, w_blk, w_kr)


def _qkv_kernel(cq_ref, ckv_ref, kr_ref, pos_ref, invf_ref, qn_ref, kvn_ref, wuq_ref, wukv_ref,
                q_ref, k_ref, v_ref, *, tm):
    ang = pos_ref[...].astype(F32) * invf_ref[...]
    cos = jnp.cos(ang)
    sin = jnp.sin(ang)
    lane = lax.broadcasted_iota(jnp.int32, (tm, LANES), 1)
    half = QK_ROPE // 2
    c_tab = jnp.where(lane < QK_ROPE, cos, 0.0)
    s_lo = jnp.where(lane < half, -sin, 0.0)
    s_hi = jnp.where((lane >= half) & (lane < QK_ROPE), sin, 0.0)

    def rope(t, tabs):
        c, lo, hi = tabs
        return t * c + pltpu.roll(t, LANES - half, 1) * lo + pltpu.roll(t, half, 1) * hi

    k_tabs = (c_tab, s_lo, s_hi)
    q_tabs = tuple(t * Q_SCALE for t in k_tabs)
    cqn = _rms(cq_ref[...].astype(F32), qn_ref[...]).astype(BF16)
    q = jnp.dot(cqn, wuq_ref[...], preferred_element_type=F32)
    nope_cols = MLA_HEADS * QK_NOPE
    for h in range(MLA_HEADS):
        q_ref[:, h * HEAD_PAD:h * HEAD_PAD + QK_NOPE] = q[:, h * QK_NOPE:(h + 1) * QK_NOPE].astype(BF16)
        q_ref[:, h * HEAD_PAD + QK_NOPE:(h + 1) * HEAD_PAD] = (
            rope(q[:, nope_cols + h * LANES:nope_cols + (h + 1) * LANES], q_tabs)).astype(BF16)

    ckvn = _rms(ckv_ref[...].astype(F32), kvn_ref[...]).astype(BF16)
    kv = jnp.dot(ckvn, wukv_ref[...], preferred_element_type=F32)
    k_rope = rope(kr_ref[...], k_tabs).astype(BF16)
    ones = jnp.ones((tm, V_HEAD), BF16)
    for h in range(MLA_HEADS):
        k_ref[:, h * HEAD_PAD:h * HEAD_PAD + QK_NOPE] = kv[:, h * HEAD_PAD:h * HEAD_PAD + QK_NOPE].astype(BF16)
        k_ref[:, h * HEAD_PAD + QK_NOPE:(h + 1) * HEAD_PAD] = k_rope
        v_ref[:, 2 * h * V_HEAD:(2 * h + 1) * V_HEAD] = kv[:, h * HEAD_PAD + QK_NOPE:(h + 1) * HEAD_PAD].astype(BF16)
        v_ref[:, (2 * h + 1) * V_HEAD:(2 * h + 2) * V_HEAD] = ones


def _qkv(z, kr, pos, invf, q_norm, kv_norm, w_uq_p, w_ukv, *, tm=512):
    T = z.shape[0]
    qk_cols = MLA_HEADS * HEAD_PAD
    return pl.pallas_call(
        functools.partial(_qkv_kernel, tm=tm),
        grid=(T // tm,),
        in_specs=[
            pl.BlockSpec((tm, Q_LORA), lambda i: (i, Z_CQ // Q_LORA)),
            pl.BlockSpec((tm, KV_LORA), lambda i: (i, Z_CKV // KV_LORA)),
            pl.BlockSpec((tm, LANES), lambda i: (i, 0)),
            pl.BlockSpec((tm, 1), lambda i: (i, 0)),
            pl.BlockSpec((1, LANES), lambda i: (0, 0)),
            pl.BlockSpec((1, Q_LORA), lambda i: (0, 0)),
            pl.BlockSpec((1, KV_LORA), lambda i: (0, 0)),
            pl.BlockSpec((Q_LORA, qk_cols), lambda i: (0, 0)),
            pl.BlockSpec((KV_LORA, qk_cols), lambda i: (0, 0)),
        ],
        out_specs=[
            pl.BlockSpec((tm, qk_cols), lambda i: (i, 0)),
            pl.BlockSpec((tm, qk_cols), lambda i: (i, 0)),
            pl.BlockSpec((tm, 2 * MLA_HEADS * V_HEAD), lambda i: (i, 0)),
        ],
        out_shape=[
            jax.ShapeDtypeStruct((T, qk_cols), BF16),
            jax.ShapeDtypeStruct((T, qk_cols), BF16),
            jax.ShapeDtypeStruct((T, 2 * MLA_HEADS * V_HEAD), BF16),
        ],
        compiler_params=_params("parallel"),
        name="qkv",
    )(z, z, kr, pos, invf, q_norm, kv_norm, w_uq_p, w_ukv)


def _mla_kernel(q_ref, k_ref, v_ref, o_ref, *, seq, tq, heads):
    for h in range(heads):
        k = k_ref[:, h * HEAD_PAD:(h + 1) * HEAD_PAD]
        v = v_ref[:, 2 * h * V_HEAD:2 * (h + 1) * V_HEAD]
        for c in range(seq // tq):
            r = slice(c * tq, (c + 1) * tq)
            s = lax.dot_general(q_ref[r, h * HEAD_PAD:(h + 1) * HEAD_PAD], k, (((1,), (1,)), ((), ())),
                                preferred_element_type=F32)
            m = jnp.max(s, axis=-1, keepdims=True)
            p = jnp.exp2(s - m).astype(BF16)
            ov = jnp.dot(p, v, preferred_element_type=F32)
            o_ref[r, h * V_HEAD:(h + 1) * V_HEAD] = (ov[:, :V_HEAD] / ov[:, V_HEAD:]).astype(BF16)


def _mla(q, k, v, *, batch, seq, tq=512, heads=4):
    T = q.shape[0]
    return pl.pallas_call(
        functools.partial(_mla_kernel, seq=seq, tq=tq, heads=heads),
        grid=(batch, MLA_HEADS // heads),
        in_specs=[
            pl.BlockSpec((seq, heads * HEAD_PAD), lambda b, h: (b, h)),
            pl.BlockSpec((seq, heads * HEAD_PAD), lambda b, h: (b, h)),
            pl.BlockSpec((seq, heads * 2 * V_HEAD), lambda b, h: (b, h)),
        ],
        out_specs=pl.BlockSpec((seq, heads * V_HEAD), lambda b, h: (b, h)),
        out_shape=jax.ShapeDtypeStruct((T, MLA_HEADS * V_HEAD), BF16),
        compiler_params=_params("parallel", "parallel"),
        name="mla_attn",
    )(q, k, v)


def _mem_kv_kernel(m_ref, g_ref, w_ref, o_ref, h_ref, *, tm):
    @pl.when(pl.program_id(1) == 0)
    def _():
        _rms_rows_to(m_ref, g_ref, h_ref, tm, 128)

    o_ref[...] = jnp.dot(h_ref[...], w_ref[...], preferred_element_type=F32).astype(BF16)


def _mem_kv(mem2, g, w, *, tm=1024, tn=1024):
    R = mem2.shape[0]
    N = w.shape[1]
    return pl.pallas_call(
        functools.partial(_mem_kv_kernel, tm=tm),
        grid=(R // tm, N // tn),
        in_specs=[
            pl.BlockSpec((tm, D_MODEL), lambda i, j: (i, 0)),
            pl.BlockSpec((1, D_MODEL), lambda i, j: (0, 0)),
            pl.BlockSpec((D_MODEL, tn), lambda i, j: (0, j)),
        ],
        out_specs=pl.BlockSpec((tm, tn), lambda i, j: (i, j)),
        out_shape=jax.ShapeDtypeStruct((R, N), BF16),
        scratch_shapes=[pltpu.VMEM((tm, D_MODEL), BF16)],
        compiler_params=_params("parallel", "arbitrary"),
        name="mem_kv",
    )(mem2, g, w)


def _mem_attn_kernel(q_ref, kv_ref, o_ref):
    scale = X_HEAD_DIM ** -0.5
    for h in range(X_HEADS):
        c = slice(h * X_HEAD_DIM, (h + 1) * X_HEAD_DIM)
        k = kv_ref[:, h * X_HEAD_DIM:(h + 1) * X_HEAD_DIM]
        v = kv_ref[:, X_WIDTH + h * X_HEAD_DIM:X_WIDTH + (h + 1) * X_HEAD_DIM]
        s = lax.dot_general(q_ref[:, c], k, (((1,), (1,)), ((), ())), preferred_element_type=F32) * scale
        m = jnp.max(s, axis=-1, keepdims=True)
        p = jnp.exp(s - m)
        l = jnp.sum(p, axis=-1, keepdims=True)
        o = jnp.dot(p.astype(BF16), v, preferred_element_type=F32)
        o_ref[:, c] = (o / l).astype(BF16)


def _mem_attn(z, kvm, *, seq, mem_len, tm=1024):
    T = z.shape[0]
    return pl.pallas_call(
        _mem_attn_kernel,
        grid=(T // tm,),
        in_specs=[
            pl.BlockSpec((tm, X_WIDTH), lambda i: (i, Z_QX // X_WIDTH)),
            pl.BlockSpec((mem_len, 2 * X_WIDTH), lambda i: (i // (seq // tm), 0)),
        ],
        out_specs=pl.BlockSpec((tm, X_WIDTH), lambda i: (i, 0)),
        out_shape=jax.ShapeDtypeStruct((T, X_WIDTH), BF16),
        compiler_params=_params("parallel"),
        name="mem_attn",
    )(z, kvm)


def _merge_kernel(o_ref, cv_ref, cb_ref, cc_ref, cvp_ref, ccp_ref, cvn_ref, ccn_ref, yx_ref,
                  g0_ref, g1_ref, g2_ref, b_ref, cw_ref,
                  wa_ref, wc_ref, wm_ref, out_ref, pext_ref, *, tm, tn, tiles_per_seq):
    pos_in_seq = pl.program_id(0) % tiles_per_seq
    keep_prev = jnp.where(pos_in_seq == 0, 0.0, 1.0)
    keep_next = jnp.where(pos_in_seq == tiles_per_seq - 1, 0.0, 1.0)
    h8 = SUBLANES_F32
    prev = (cvp_ref[h8:, :].astype(F32) * ccp_ref[h8:, :].astype(F32)) * keep_prev
    nxt = (cvn_ref[:h8, :].astype(F32) * ccn_ref[:h8, :].astype(F32)) * keep_next
    pext_ref[0:h8, :] = prev
    pext_ref[tm + h8:tm + 2 * h8, :] = nxt
    pext_ref[h8:tm + h8, :] = cv_ref[...].astype(F32) * cc_ref[...].astype(F32)
    conv = (cw_ref[0:1, :] * pext_ref[h8 - 1:tm + h8 - 1, :] + cw_ref[1:2, :] * pext_ref[h8:tm + h8, :]
            + cw_ref[2:3, :] * pext_ref[h8 + 1:tm + h8 + 1, :])
    cbr = (cb_ref[...].astype(F32) * conv).astype(BF16)

    def gate(g_ref, k, c):
        return jax.nn.sigmoid(g_ref[:, c].astype(F32) + b_ref[:, k * D_MODEL + c.start:k * D_MODEL + c.stop])

    for n in range(D_MODEL // tn):
        c = slice(n * tn, (n + 1) * tn)
        y = gate(g0_ref, 0, c) * jnp.dot(o_ref[...], wa_ref[:, c], preferred_element_type=F32)
        y += gate(g1_ref, 1, c) * jnp.dot(cbr, wc_ref[:, c], preferred_element_type=F32)
        y += gate(g2_ref, 2, c) * jnp.dot(yx_ref[...], wm_ref[:, c], preferred_element_type=F32)
        out_ref[:, c] = y.astype(BF16)


def _merge(o_mla, z, yx, gate_bias, conv_w, w_o_mla, w_out_conv, w_o_mem, *, seq, tm=512, tn=2048):
    T = z.shape[0]
    hb = SUBLANES_BF16
    rb = tm // hb
    last_hb = T // hb - 1
    cw = CONV_WIDTH
    once = pl.Buffered(1)

    def zcol(c0, width):
        return c0 // width

    def gate_spec(k):
        return pl.BlockSpec((tm, D_MODEL), lambda i, k=k: (i, Z_G // D_MODEL + k))

    def prev_spec(c0):
        return pl.BlockSpec((hb, cw), lambda i: (jnp.maximum(i * rb - 1, 0), zcol(c0, cw)))

    def next_spec(c0):
        return pl.BlockSpec((hb, cw), lambda i: (jnp.minimum((i + 1) * rb, last_hb), zcol(c0, cw)))

    def w_spec():
        return pl.BlockSpec((cw, D_MODEL), lambda i: (0, 0), pipeline_mode=once)

    assert Z_G % D_MODEL == 0
    return pl.pallas_call(
        functools.partial(_merge_kernel, tm=tm, tn=tn, tiles_per_seq=seq // tm),
        grid=(T // tm,),
        in_specs=[
            pl.BlockSpec((tm, MLA_HEADS * V_HEAD), lambda i: (i, 0)),
            pl.BlockSpec((tm, cw), lambda i: (i, zcol(Z_CV, cw))),
            pl.BlockSpec((tm, cw), lambda i: (i, zcol(Z_CB, cw))),
            pl.BlockSpec((tm, cw), lambda i: (i, zcol(Z_CC, cw))),
            prev_spec(Z_CV), prev_spec(Z_CC), next_spec(Z_CV), next_spec(Z_CC),
            pl.BlockSpec((tm, X_WIDTH), lambda i: (i, 0)),
            gate_spec(0), gate_spec(1), gate_spec(2),
            pl.BlockSpec((1, N_BRANCH * D_MODEL), lambda i: (0, 0), pipeline_mode=once),
            pl.BlockSpec((3, cw), lambda i: (0, 0), pipeline_mode=once),
            w_spec(), w_spec(), w_spec(),
        ],
        out_specs=pl.BlockSpec((tm, D_MODEL), lambda i: (i, 0)),
        out_shape=jax.ShapeDtypeStruct((T, D_MODEL), BF16),
        scratch_shapes=[pltpu.VMEM((tm + 2 * SUBLANES_F32, cw), F32)],
        compiler_params=_params("parallel"),
        name="merge",
    )(o_mla, z, z, z, z, z, z, z, yx, z, z, z, gate_bias, conv_w, w_o_mla, w_out_conv, w_o_mem)


def _out_proj_kernel(m_ref, w_ref, x_ref, o_ref):
    o_ref[...] = x_ref[...] + jnp.dot(m_ref[...], w_ref[...], preferred_element_type=F32)


def _out_proj(merged, w_o, x2, *, tm=512):
    T = x2.shape[0]
    return pl.pallas_call(
        _out_proj_kernel,
        grid=(T // tm,),
        in_specs=[
            pl.BlockSpec((tm, D_MODEL), lambda i: (i, 0)),
            pl.BlockSpec((D_MODEL, D_MODEL), lambda i: (0, 0), pipeline_mode=pl.Buffered(1)),
            pl.BlockSpec((tm, D_MODEL), lambda i: (i, 0)),
        ],
        out_specs=pl.BlockSpec((tm, D_MODEL), lambda i: (i, 0)),
        out_shape=jax.ShapeDtypeStruct((T, D_MODEL), F32),
        compiler_params=_params("parallel"),
        name="out_proj",
    )(merged, w_o, x2)


FFN_COL_CHUNK = 256


def _ffn_kernel(x_ref, xp_ref, xn_ref, g_ref, wu_ref, ct_ref, wd_ref, fg_ref,
                o_ref, hn_ref, u_ref, *, tm, tf, tiles_per_seq, n_f, bps, final):
    acc_ref = o_ref
    i = pl.program_id(0)
    s = pl.program_id(1)
    h8 = SUBLANES_F32
    cw = FFN_COL_CHUNK
    rows = 128
    n_chunks = tf // cw
    n_full = n_f // bps
    n_steps = pl.cdiv(n_f, bps)

    @pl.when(s == 0)
    def _():
        pos_in_seq = i % tiles_per_seq
        keep_prev = jnp.where(pos_in_seq == 0, 0.0, 1.0)
        keep_next = jnp.where(pos_in_seq == tiles_per_seq - 1, 0.0, 1.0)
        halo = jnp.concatenate([_rms(xp_ref[...], g_ref[...]) * keep_prev,
                                _rms(xn_ref[...], g_ref[...]) * keep_next], axis=0)
        hn_ref[tm:tm + 2 * h8, :] = halo.astype(BF16)
        _rms_rows_to(x_ref, g_ref, hn_ref, tm, rows)
        acc_ref[...] = jnp.zeros_like(acc_ref)

    def up(fb, u_ref):
        r = jnp.dot(hn_ref[...], wu_ref[fb], preferred_element_type=F32)
        u_ref[h8:tm + h8, :] = r[:tm]
        u_ref[0:h8, :] = r[tm:tm + h8]
        u_ref[tm + h8:tm + 2 * h8, :] = r[tm + h8:]

    def conv(u_ref, fb, c0):
        return (ct_ref[fb, 0:1, c0:c0 + cw] * u_ref[h8 - 1:tm + h8 - 1, c0:c0 + cw]
                + ct_ref[fb, 1:2, c0:c0 + cw] * u_ref[h8:tm + h8, c0:c0 + cw]
                + ct_ref[fb, 2:3, c0:c0 + cw] * u_ref[h8 + 1:tm + h8 + 1, c0:c0 + cw])

    def block(fb):
        up(fb, u_ref.at[fb])
        down = None
        for n in range(n_chunks):
            c0 = n * cw
            a = conv(u_ref.at[fb], fb, c0)
            b = conv(u_ref.at[fb], fb, tf + c0)
            act = (a * jax.nn.sigmoid(a) * b).astype(BF16)
            d = jnp.dot(act, wd_ref[fb * tf + c0:fb * tf + c0 + cw, :], preferred_element_type=F32)
            down = d if down is None else down + d
        acc_ref[...] += down

    def blocks(count):
        for fb in range(count):
            block(fb)

    pl.when(s < n_full)(functools.partial(blocks, bps))
    if n_f % bps:
        pl.when(s == n_full)(functools.partial(blocks, n_f % bps))

    @pl.when(s == n_steps - 1)
    def _():
        def body(c, carry):
            r = pl.ds(pl.multiple_of(c * rows, rows), rows)
            y = x_ref[r, :] + acc_ref[r, :]
            o_ref[r, :] = _rms(y, fg_ref[...]) if final else y
            return carry
        lax.fori_loop(0, tm // rows, body, 0)


def _ffn(x1, ffn_norm, w_up_blk, conv_taps, w_down, final_norm, *, seq, final, tm=512, bps=2):
    T = x1.shape[0]
    h8 = SUBLANES_F32
    rb = tm // h8
    last_hb = T // h8 - 1
    n_f = w_up_blk.shape[0]
    tf = w_up_blk.shape[2] // 2
    return pl.pallas_call(
        functools.partial(_ffn_kernel, tm=tm, tf=tf, tiles_per_seq=seq // tm, n_f=n_f, bps=bps, final=final),
        grid=(T // tm, pl.cdiv(n_f, bps)),
        in_specs=[
            pl.BlockSpec((tm, D_MODEL), lambda i, s: (i, 0)),
            pl.BlockSpec((h8, D_MODEL), lambda i, s: (jnp.maximum(i * rb - 1, 0), 0)),
            pl.BlockSpec((h8, D_MODEL), lambda i, s: (jnp.minimum((i + 1) * rb, last_hb), 0)),
            pl.BlockSpec((1, D_MODEL), lambda i, s: (0, 0)),
            pl.BlockSpec((bps, D_MODEL, 2 * tf), lambda i, s: (s, 0, 0)),
            pl.BlockSpec((bps, 3, 2 * tf), lambda i, s: (s, 0, 0)),
            pl.BlockSpec((bps * tf, D_MODEL), lambda i, s: (s, 0)),
            pl.BlockSpec((1, D_MODEL), lambda i, s: (0, 0)),
        ],
        out_specs=pl.BlockSpec((tm, D_MODEL), lambda i, s: (i, 0)),
        out_shape=jax.ShapeDtypeStruct((T, D_MODEL), F32),
        scratch_shapes=[
            pltpu.VMEM((tm + 2 * h8, D_MODEL), BF16),
            pltpu.VMEM((bps, tm + 2 * h8, 2 * tf), F32),
        ],
        compiler_params=_params("parallel", "arbitrary"),
        name="ffn",
    )(x1, x1, x1, ffn_norm, w_up_blk, conv_taps, w_down, final_norm)


def _block_conv_taps(ffn_conv_w):
    k = ffn_conv_w.shape[0]
    n_f = D_FF // FFN_TF
    halves = ffn_conv_w.reshape(k, 2, n_f, FFN_TF)
    return jnp.transpose(halves, (2, 0, 1, 3)).reshape(n_f, k, 2 * FFN_TF)


def _pack_w_uq(w_uq):
    w = w_uq.reshape(Q_LORA, MLA_HEADS, QK_NOPE + QK_ROPE)
    nope = (w[:, :, :QK_NOPE] * Q_SCALE).reshape(Q_LORA, MLA_HEADS * QK_NOPE)
    rope = jnp.pad(w[:, :, QK_NOPE:], ((0, 0), (0, 0), (0, LANES - QK_ROPE))).reshape(Q_LORA, MLA_HEADS * LANES)
    return jnp.concatenate([nope, rope], axis=1).astype(BF16)


def kernel(x, mem, positions, mix_norm, w_in, q_norm, w_uq, kv_norm, w_ukv, w_o_mla, conv_w, w_out_conv,
           mem_norm, w_mem_kv, w_o_mem, gate_bias, w_o, ffn_norm, w_up, ffn_conv_w, w_down, final_norm):
    B, S, D = x.shape
    M = mem.shape[1]
    T = B * S
    depth = w_in.shape[0]
    x2 = x.reshape(T, D)
    mem2 = mem.reshape(B * M, D)
    pos = positions.reshape(T, 1)
    inv_freq = jnp.power(ROPE_THETA, -jnp.arange(0, QK_ROPE, 2, dtype=F32) / QK_ROPE)
    invf = jnp.concatenate([inv_freq, inv_freq, jnp.zeros((LANES - QK_ROPE,), F32)]).reshape(1, LANES)

    for l in range(depth):
        w_blk, w_kr = _pack_w_in(jnp.swapaxes(w_in[l], 0, 1))
        z, kr = _in_proj(x2, mix_norm[l].reshape(1, D), w_blk, w_kr)
        q, k, v = _qkv(z, kr, pos, invf, q_norm[l].reshape(1, -1), kv_norm[l].reshape(1, -1),
                       _pack_w_uq(w_uq[l]), w_ukv[l].astype(BF16))
        o_mla = _mla(q, k, v, batch=B, seq=S)
        kvm = _mem_kv(mem2, mem_norm[l].reshape(1, D), w_mem_kv[l].astype(BF16))
        yx = _mem_attn(z, kvm, seq=S, mem_len=M)
        merged = _merge(o_mla, z, yx, gate_bias[l].reshape(1, -1), conv_w[l],
                        w_o_mla[l].astype(BF16), w_out_conv[l].astype(BF16), w_o_mem[l].astype(BF16), seq=S)
        x2 = _out_proj(merged, w_o[l].astype(BF16), x2)
        x2 = _ffn(x2, ffn_norm[l].reshape(1, D), _pack_w_up(w_up[l]), _block_conv_taps(ffn_conv_w[l]),
                  w_down[l].astype(BF16), final_norm.reshape(1, D), seq=S, final=(l == depth - 1))
    return x2.reshape(B, S, D)
```

```python
import functools

import jax
import jax.numpy as jnp
from jax import lax
from jax.experimental import pallas as pl
from jax.experimental.pallas import tpu as pltpu

F32 = jnp.float32
BF16 = jnp.bfloat16

D_MODEL = 2048
MLA_HEADS = 8
Q_LORA = 512
KV_LORA = 512
QK_NOPE = 128
QK_ROPE = 64
V_HEAD = 128
ROPE_THETA = 10000.0
CONV_WIDTH = 1024
X_HEADS = 4
X_HEAD_DIM = 256
X_WIDTH = X_HEADS * X_HEAD_DIM
D_FF = 5632
N_BRANCH = 3
EPS = 1e-6
LOG2_E = 1.4426950408889634
Q_SCALE = (QK_NOPE + QK_ROPE) ** -0.5 * LOG2_E

LANES = 128
SUBLANES_F32 = 8
SUBLANES_BF16 = 16
HEAD_PAD = 256
VMEM_LIMIT = 56 * 1024 * 1024

Z_G = 0
Z_CQ = Z_G + N_BRANCH * D_MODEL
Z_CKV = Z_CQ + Q_LORA
Z_CV = Z_CKV + KV_LORA
Z_CB = Z_CV + CONV_WIDTH
Z_CC = Z_CB + CONV_WIDTH
Z_QX = Z_CC + CONV_WIDTH
Z_COLS = Z_QX + X_WIDTH


_NT = (((1,), (1,)), ((), ()))


def _params(*sem):
    return pltpu.CompilerParams(dimension_semantics=sem, vmem_limit_bytes=VMEM_LIMIT)


def _rms(x, g):
    inv = lax.rsqrt(jnp.mean(x * x, axis=-1, keepdims=True) + EPS)
    return x * inv * g


def _rms_rows_to(x_ref, g_ref, out_ref, rows, chunk, out_row0=0):
    def body(c, carry):
        r0 = pl.multiple_of(c * chunk, chunk)
        x = x_ref[pl.ds(r0, chunk), :].astype(F32)
        out_ref[pl.ds(out_row0 + r0, chunk), :] = _rms(x, g_ref[...]).astype(out_ref.dtype)
        return carry
    lax.fori_loop(0, rows // chunk, body, 0)


IN_TN = 1024
FFN_TF = 512


N_GATE_BLK = N_BRANCH * D_MODEL // IN_TN


def _pack_w_in_kernel(a_ref, b_ref, kr_ref, wblk_ref, wkr_ref):
    j = pl.program_id(0)

    @pl.when(j == N_GATE_BLK)
    def _():
        wblk_ref[0] = a_ref[...].astype(BF16)

    @pl.when(j != N_GATE_BLK)
    def _():
        wblk_ref[0, :IN_TN - QK_ROPE, :] = a_ref[QK_ROPE:, :].astype(BF16)
        wblk_ref[0, IN_TN - QK_ROPE:, :] = b_ref[...].astype(BF16)

    @pl.when(j == 0)
    def _():
        wkr_ref[:QK_ROPE, :] = kr_ref[...].astype(BF16)
        wkr_ref[QK_ROPE:, :] = jnp.zeros((LANES - QK_ROPE, wkr_ref.shape[1]), BF16)


def _pack_w_in(wi_t):
    cols, d = wi_t.shape
    kr0 = Q_LORA + KV_LORA
    assert cols == Z_COLS + QK_ROPE and kr0 == IN_TN and IN_TN % QK_ROPE == 0
    n_blk = Z_COLS // IN_TN
    n_tail = n_blk - 1 - N_GATE_BLK
    sub = IN_TN // QK_ROPE

    def src_blk(j):
        return jnp.where(j < N_GATE_BLK, j + 1 + n_tail, jnp.where(j == N_GATE_BLK, 0, j - N_GATE_BLK))

    return pl.pallas_call(
        _pack_w_in_kernel,
        grid=(n_blk,),
        in_specs=[
            pl.BlockSpec((IN_TN, d), lambda j: (src_blk(j), 0)),
            pl.BlockSpec((QK_ROPE, d), lambda j: ((src_blk(j) + 1) * sub, 0)),
            pl.BlockSpec((QK_ROPE, d), lambda j: (kr0 // QK_ROPE, 0)),
        ],
        out_specs=[
            pl.BlockSpec((1, IN_TN, d), lambda j: (j, 0, 0)),
            pl.BlockSpec((LANES, d), lambda j: (0, 0)),
        ],
        out_shape=[
            jax.ShapeDtypeStruct((n_blk, IN_TN, d), BF16),
            jax.ShapeDtypeStruct((LANES, d), BF16),
        ],
        compiler_params=_params("arbitrary"),
        name="pack_w_in",
    )(wi_t, wi_t, wi_t)


def _pack_w_up_kernel(w_ref, o_ref):
    o_ref[0] = w_ref[...].astype(BF16)


def _pack_w_up(w_up):
    d = w_up.shape[0]
    n_f = D_FF // FFN_TF
    return pl.pallas_call(
        _pack_w_up_kernel,
        grid=(n_f, 2),
        in_specs=[pl.BlockSpec((d, FFN_TF), lambda f, h: (0, h * n_f + f))],
        out_specs=pl.BlockSpec((1, d, FFN_TF), lambda f, h: (f, 0, h)),
        out_shape=jax.ShapeDtypeStruct((n_f, d, 2 * FFN_TF), BF16),
        compiler_params=_params("parallel", "parallel"),
        name="pack_w_up",
    )(w_up)


def _cast_kernel(*refs):
    n = len(refs) // 2
    for w_ref, o_ref in zip(refs[:n], refs[n:]):
        o_ref[...] = w_ref[...].astype(BF16)


def _cast_bf16(*ws, steps=16):
    return pl.pallas_call(
        _cast_kernel,
        grid=(steps,),
        in_specs=[pl.BlockSpec((w.shape[0] // steps, w.shape[1]), lambda i: (i, 0)) for w in ws],
        out_specs=[pl.BlockSpec((w.shape[0] // steps, w.shape[1]), lambda i: (i, 0)) for w in ws],
        out_shape=[jax.ShapeDtypeStruct(w.shape, BF16) for w in ws],
        compiler_params=_params("parallel"),
        name="cast_weights",
    )(*ws)


def _in_proj_kernel(x_ref, g_ref, w_ref, wkr_ref, z_ref, kr_ref, h_ref, *, tm):
    @pl.when(pl.program_id(1) == 0)
    def _():
        _rms_rows_to(x_ref, g_ref, h_ref, tm, 128)
        kr_ref[...] = lax.dot_general(h_ref[...], wkr_ref[...], _NT, preferred_element_type=F32)

    z_ref[...] = lax.dot_general(h_ref[...], w_ref[0], _NT, preferred_element_type=F32).astype(BF16)


def _in_proj(x2, g, w_blk, w_kr, *, tm=1024):
    T = x2.shape[0]
    tn = IN_TN
    return pl.pallas_call(
        functools.partial(_in_proj_kernel, tm=tm),
        grid=(T // tm, Z_COLS // tn),
        in_specs=[
            pl.BlockSpec((tm, D_MODEL), lambda i, j: (i, 0)),
            pl.BlockSpec((1, D_MODEL), lambda i, j: (0, 0)),
            pl.BlockSpec((1, tn, D_MODEL), lambda i, j: (j, 0, 0)),
            pl.BlockSpec((LANES, D_MODEL), lambda i, j: (0, 0)),
        ],
        out_specs=[
            pl.BlockSpec((tm, tn), lambda i, j: (i, j)),
            pl.BlockSpec((tm, LANES), lambda i, j: (i, 0)),
        ],
        out_shape=[
            jax.ShapeDtypeStruct((T, Z_COLS), BF16),
            jax.ShapeDtypeStruct((T, LANES), F32),
        ],
        scratch_shapes=[pltpu.VMEM((tm, D_MODEL), BF16)],
        compiler_params=_params("parallel", "arbitrary"),
        name="in_proj",
    )(x2, g, w_blk, w_kr)


def _qkv_kernel(cq_ref, ckv_ref, kr_ref, pos_ref, invf_ref, qn_ref, kvn_ref, wuq_ref, wukv_ref,
                q_ref, k_ref, v_ref, *, tm):
    ang = pos_ref[...].astype(F32) * invf_ref[...]
    cos = jnp.cos(ang)
    sin = jnp.sin(ang)
    lane = lax.broadcasted_iota(jnp.int32, (tm, LANES), 1)
    half = QK_ROPE // 2
    c_tab = jnp.where(lane < QK_ROPE, cos, 0.0)
    s_lo = jnp.where(lane < half, -sin, 0.0)
    s_hi = jnp.where((lane >= half) & (lane < QK_ROPE), sin, 0.0)

    def rope(t, tabs):
        c, lo, hi = tabs
        return t * c + pltpu.roll(t, LANES - half, 1) * lo + pltpu.roll(t, half, 1) * hi

    k_tabs = (c_tab, s_lo, s_hi)
    q_tabs = tuple(t * Q_SCALE for t in k_tabs)
    cqn = _rms(cq_ref[...].astype(F32), qn_ref[...]).astype(BF16)
    q = jnp.dot(cqn, wuq_ref[...], preferred_element_type=F32)
    nope_cols = MLA_HEADS * QK_NOPE
    for h in range(MLA_HEADS):
        q_ref[:, h * HEAD_PAD:h * HEAD_PAD + QK_NOPE] = q[:, h * QK_NOPE:(h + 1) * QK_NOPE].astype(BF16)
        q_ref[:, h * HEAD_PAD + QK_NOPE:(h + 1) * HEAD_PAD] = (
            rope(q[:, nope_cols + h * LANES:nope_cols + (h + 1) * LANES], q_tabs)).astype(BF16)

    ckvn = _rms(ckv_ref[...].astype(F32), kvn_ref[...]).astype(BF16)
    kv = jnp.dot(ckvn, wukv_ref[...], preferred_element_type=F32)
    k_rope = rope(kr_ref[...], k_tabs).astype(BF16)
    ones = jnp.ones((tm, V_HEAD), BF16)
    for h in range(MLA_HEADS):
        k_ref[:, h * HEAD_PAD:h * HEAD_PAD + QK_NOPE] = kv[:, h * HEAD_PAD:h * HEAD_PAD + QK_NOPE].astype(BF16)
        k_ref[:, h * HEAD_PAD + QK_NOPE:(h + 1) * HEAD_PAD] = k_rope
        v_ref[:, 2 * h * V_HEAD:(2 * h + 1) * V_HEAD] = kv[:, h * HEAD_PAD + QK_NOPE:(h + 1) * HEAD_PAD].astype(BF16)
        v_ref[:, (2 * h + 1) * V_HEAD:(2 * h + 2) * V_HEAD] = ones


def _qkv(z, kr, pos, invf, q_norm, kv_norm, w_uq_p, w_ukv, *, tm=512):
    T = z.shape[0]
    qk_cols = MLA_HEADS * HEAD_PAD
    return pl.pallas_call(
        functools.partial(_qkv_kernel, tm=tm),
        grid=(T // tm,),
        in_specs=[
            pl.BlockSpec((tm, Q_LORA), lambda i: (i, Z_CQ // Q_LORA)),
            pl.BlockSpec((tm, KV_LORA), lambda i: (i, Z_CKV // KV_LORA)),
            pl.BlockSpec((tm, LANES), lambda i: (i, 0)),
            pl.BlockSpec((tm, 1), lambda i: (i, 0)),
            pl.BlockSpec((1, LANES), lambda i: (0, 0)),
            pl.BlockSpec((1, Q_LORA), lambda i: (0, 0)),
            pl.BlockSpec((1, KV_LORA), lambda i: (0, 0)),
            pl.BlockSpec((Q_LORA, qk_cols), lambda i: (0, 0)),
            pl.BlockSpec((KV_LORA, qk_cols), lambda i: (0, 0)),
        ],
        out_specs=[
            pl.BlockSpec((tm, qk_cols), lambda i: (i, 0)),
            pl.BlockSpec((tm, qk_cols), lambda i: (i, 0)),
            pl.BlockSpec((tm, 2 * MLA_HEADS * V_HEAD), lambda i: (i, 0)),
        ],
        out_shape=[
            jax.ShapeDtypeStruct((T, qk_cols), BF16),
            jax.ShapeDtypeStruct((T, qk_cols), BF16),
            jax.ShapeDtypeStruct((T, 2 * MLA_HEADS * V_HEAD), BF16),
        ],
        compiler_params=_params("parallel"),
        name="qkv",
    )(z, z, kr, pos, invf, q_norm, kv_norm, w_uq_p, w_ukv)


def _mla_kernel(q_ref, k_ref, v_ref, o_ref, *, seq, tq, heads):
    for h in range(heads):
        k = k_ref[:, h * HEAD_PAD:(h + 1) * HEAD_PAD]
        v = v_ref[:, 2 * h * V_HEAD:2 * (h + 1) * V_HEAD]
        for c in range(seq // tq):
            r = slice(c * tq, (c + 1) * tq)
            s = lax.dot_general(q_ref[r, h * HEAD_PAD:(h + 1) * HEAD_PAD], k, (((1,), (1,)), ((), ())),
                                preferred_element_type=F32)
            m = jnp.max(s, axis=-1, keepdims=True)
            p = jnp.exp2(s - m).astype(BF16)
            ov = jnp.dot(p, v, preferred_element_type=F32)
            o_ref[r, h * V_HEAD:(h + 1) * V_HEAD] = (ov[:, :V_HEAD] / ov[:, V_HEAD:]).astype(BF16)


def _mla(q, k, v, *, batch, seq, tq=512, heads=4):
    T = q.shape[0]
    return pl.pallas_call(
        functools.partial(_mla_kernel, seq=seq, tq=tq, heads=heads),
        grid=(batch, MLA_HEADS // heads),
        in_specs=[
            pl.BlockSpec((seq, heads * HEAD_PAD), lambda b, h: (b, h)),
            pl.BlockSpec((seq, heads * HEAD_PAD), lambda b, h: (b, h)),
            pl.BlockSpec((seq, heads * 2 * V_HEAD), lambda b, h: (b, h)),
        ],
        out_specs=pl.BlockSpec((seq, heads * V_HEAD), lambda b, h: (b, h)),
        out_shape=jax.ShapeDtypeStruct((T, MLA_HEADS * V_HEAD), BF16),
        compiler_params=_params("parallel", "parallel"),
        name="mla_attn",
    )(q, k, v)


def _mem_kv_kernel(m_ref, g_ref, w_ref, o_ref, h_ref, *, tm):
    @pl.when(pl.program_id(1) == 0)
    def _():
        _rms_rows_to(m_ref, g_ref, h_ref, tm, 128)

    o_ref[...] = jnp.dot(h_ref[...], w_ref[...], preferred_element_type=F32).astype(BF16)


def _mem_kv(mem2, g, w, *, tm=1024, tn=1024):
    R = mem2.shape[0]
    N = w.shape[1]
    return pl.pallas_call(
        functools.partial(_mem_kv_kernel, tm=tm),
        grid=(R // tm, N // tn),
        in_specs=[
            pl.BlockSpec((tm, D_MODEL), lambda i, j: (i, 0)),
            pl.BlockSpec((1, D_MODEL), lambda i, j: (0, 0)),
            pl.BlockSpec((D_MODEL, tn), lambda i, j: (0, j)),
        ],
        out_specs=pl.BlockSpec((tm, tn), lambda i, j: (i, j)),
        out_shape=jax.ShapeDtypeStruct((R, N), BF16),
        scratch_shapes=[pltpu.VMEM((tm, D_MODEL), BF16)],
        compiler_params=_params("parallel", "arbitrary"),
        name="mem_kv",
    )(mem2, g, w)


def _mem_attn_kernel(q_ref, kv_ref, o_ref):
    scale = X_HEAD_DIM ** -0.5
    for h in range(X_HEADS):
        c = slice(h * X_HEAD_DIM, (h + 1) * X_HEAD_DIM)
        k = kv_ref[:, h * X_HEAD_DIM:(h + 1) * X_HEAD_DIM]
        v = kv_ref[:, X_WIDTH + h * X_HEAD_DIM:X_WIDTH + (h + 1) * X_HEAD_DIM]
        s = lax.dot_general(q_ref[:, c], k, (((1,), (1,)), ((), ())), preferred_element_type=F32) * scale
        m = jnp.max(s, axis=-1, keepdims=True)
        p = jnp.exp(s - m)
        l = jnp.sum(p, axis=-1, keepdims=True)
        o = jnp.dot(p.astype(BF16), v, preferred_element_type=F32)
        o_ref[:, c] = (o / l).astype(BF16)


def _mem_attn(z, kvm, *, seq, mem_len, tm=1024):
    T = z.shape[0]
    return pl.pallas_call(
        _mem_attn_kernel,
        grid=(T // tm,),
        in_specs=[
            pl.BlockSpec((tm, X_WIDTH), lambda i: (i, Z_QX // X_WIDTH)),
            pl.BlockSpec((mem_len, 2 * X_WIDTH), lambda i: (i // (seq // tm), 0)),
        ],
        out_specs=pl.BlockSpec((tm, X_WIDTH), lambda i: (i, 0)),
        out_shape=jax.ShapeDtypeStruct((T, X_WIDTH), BF16),
        compiler_params=_params("parallel"),
        name="mem_attn",
    )(z, kvm)


def _merge_kernel(o_ref, cv_ref, cb_ref, cc_ref, cvp_ref, ccp_ref, cvn_ref, ccn_ref, yx_ref,
                  g0_ref, g1_ref, g2_ref, b_ref, cw_ref,
                  wa_ref, wc_ref, wm_ref, out_ref, pext_ref, *, tm, tn, tiles_per_seq):
    pos_in_seq = pl.program_id(0) % tiles_per_seq
    keep_prev = jnp.where(pos_in_seq == 0, 0.0, 1.0)
    keep_next = jnp.where(pos_in_seq == tiles_per_seq - 1, 0.0, 1.0)
    h8 = SUBLANES_F32
    prev = (cvp_ref[h8:, :].astype(F32) * ccp_ref[h8:, :].astype(F32)) * keep_prev
    nxt = (cvn_ref[:h8, :].astype(F32) * ccn_ref[:h8, :].astype(F32)) * keep_next
    pext_ref[0:h8, :] = prev
    pext_ref[tm + h8:tm + 2 * h8, :] = nxt
    pext_ref[h8:tm + h8, :] = cv_ref[...].astype(F32) * cc_ref[...].astype(F32)
    conv = (cw_ref[0:1, :] * pext_ref[h8 - 1:tm + h8 - 1, :] + cw_ref[1:2, :] * pext_ref[h8:tm + h8, :]
            + cw_ref[2:3, :] * pext_ref[h8 + 1:tm + h8 + 1, :])
    cbr = (cb_ref[...].astype(F32) * conv).astype(BF16)

    def gate(g_ref, k, c):
        return jax.nn.sigmoid(g_ref[:, c].astype(F32) + b_ref[:, k * D_MODEL + c.start:k * D_MODEL + c.stop])

    for n in range(D_MODEL // tn):
        c = slice(n * tn, (n + 1) * tn)
        y = gate(g0_ref, 0, c) * jnp.dot(o_ref[...], wa_ref[:, c], preferred_element_type=F32)
        y += gate(g1_ref, 1, c) * jnp.dot(cbr, wc_ref[:, c], preferred_element_type=F32)
        y += gate(g2_ref, 2, c) * jnp.dot(yx_ref[...], wm_ref[:, c], preferred_element_type=F32)
        out_ref[:, c] = y.astype(BF16)


def _merge(o_mla, z, yx, gate_bias, conv_w, w_o_mla, w_out_conv, w_o_mem, *, seq, tm=512, tn=2048):
    T = z.shape[0]
    hb = SUBLANES_BF16
    rb = tm // hb
    last_hb = T // hb - 1
    cw = CONV_WIDTH
    once = pl.Buffered(1)

    def zcol(c0, width):
        return c0 // width

    def gate_spec(k):
        return pl.BlockSpec((tm, D_MODEL), lambda i, k=k: (i, Z_G // D_MODEL + k))

    def prev_spec(c0):
        return pl.BlockSpec((hb, cw), lambda i: (jnp.maximum(i * rb - 1, 0), zcol(c0, cw)))

    def next_spec(c0):
        return pl.BlockSpec((hb, cw), lambda i: (jnp.minimum((i + 1) * rb, last_hb), zcol(c0, cw)))

    def w_spec():
        return pl.BlockSpec((cw, D_MODEL), lambda i: (0, 0), pipeline_mode=once)

    assert Z_G % D_MODEL == 0
    return pl.pallas_call(
        functools.partial(_merge_kernel, tm=tm, tn=tn, tiles_per_seq=seq // tm),
        grid=(T // tm,),
        in_specs=[
            pl.BlockSpec((tm, MLA_HEADS * V_HEAD), lambda i: (i, 0)),
            pl.BlockSpec((tm, cw), lambda i: (i, zcol(Z_CV, cw))),
            pl.BlockSpec((tm, cw), lambda i: (i, zcol(Z_CB, cw))),
            pl.BlockSpec((tm, cw), lambda i: (i, zcol(Z_CC, cw))),
            prev_spec(Z_CV), prev_spec(Z_CC), next_spec(Z_CV), next_spec(Z_CC),
            pl.BlockSpec((tm, X_WIDTH), lambda i: (i, 0)),
            gate_spec(0), gate_spec(1), gate_spec(2),
            pl.BlockSpec((1, N_BRANCH * D_MODEL), lambda i: (0, 0), pipeline_mode=once),
            pl.BlockSpec((3, cw), lambda i: (0, 0), pipeline_mode=once),
            w_spec(), w_spec(), w_spec(),
        ],
        out_specs=pl.BlockSpec((tm, D_MODEL), lambda i: (i, 0)),
        out_shape=jax.ShapeDtypeStruct((T, D_MODEL), BF16),
        scratch_shapes=[pltpu.VMEM((tm + 2 * SUBLANES_F32, cw), F32)],
        compiler_params=_params("parallel"),
        name="merge",
    )(o_mla, z, z, z, z, z, z, z, yx, z, z, z, gate_bias, conv_w, w_o_mla, w_out_conv, w_o_mem)


def _out_proj_kernel(m_ref, w_ref, x_ref, o_ref):
    o_ref[...] = x_ref[...] + jnp.dot(m_ref[...], w_ref[...], preferred_element_type=F32)


def _out_proj(merged, w_o, x2, *, tm=512):
    T = x2.shape[0]
    return pl.pallas_call(
        _out_proj_kernel,
        grid=(T // tm,),
        in_specs=[
            pl.BlockSpec((tm, D_MODEL), lambda i: (i, 0)),
            pl.BlockSpec((D_MODEL, D_MODEL), lambda i: (0, 0), pipeline_mode=pl.Buffered(1)),
            pl.BlockSpec((tm, D_MODEL), lambda i: (i, 0)),
        ],
        out_specs=pl.BlockSpec((tm, D_MODEL), lambda i: (i, 0)),
        out_shape=jax.ShapeDtypeStruct((T, D_MODEL), F32),
        compiler_params=_params("parallel"),
        name="out_proj",
    )(merged, w_o, x2)


FFN_COL_CHUNK = 256


def _ffn_kernel(x_ref, xp_ref, xn_ref, g_ref, wu_ref, ct_ref, wd_ref, fg_ref,
                o_ref, hn_ref, u_ref, *, tm, tf, tiles_per_seq, n_f, bps, final):
    acc_ref = o_ref
    i = pl.program_id(0)
    s = pl.program_id(1)
    h8 = SUBLANES_F32
    cw = FFN_COL_CHUNK
    rows = 128
    n_chunks = tf // cw
    n_full = n_f // bps
    n_steps = pl.cdiv(n_f, bps)

    @pl.when(s == 0)
    def _():
        pos_in_seq = i % tiles_per_seq
        keep_prev = jnp.where(pos_in_seq == 0, 0.0, 1.0)
        keep_next = jnp.where(pos_in_seq == tiles_per_seq - 1, 0.0, 1.0)
        halo = jnp.concatenate([_rms(xp_ref[...], g_ref[...]) * keep_prev,
                                _rms(xn_ref[...], g_ref[...]) * keep_next], axis=0)
        hn_ref[tm:tm + 2 * h8, :] = halo.astype(BF16)
        _rms_rows_to(x_ref, g_ref, hn_ref, tm, rows)
        acc_ref[...] = jnp.zeros_like(acc_ref)

    def up(fb, u_ref):
        r = jnp.dot(hn_ref[...], wu_ref[fb], preferred_element_type=F32)
        u_ref[h8:tm + h8, :] = r[:tm]
        u_ref[0:h8, :] = r[tm:tm + h8]
        u_ref[tm + h8:tm + 2 * h8, :] = r[tm + h8:]

    def conv(u_ref, fb, c0):
        return (ct_ref[fb, 0:1, c0:c0 + cw] * u_ref[h8 - 1:tm + h8 - 1, c0:c0 + cw]
                + ct_ref[fb, 1:2, c0:c0 + cw] * u_ref[h8:tm + h8, c0:c0 + cw]
                + ct_ref[fb, 2:3, c0:c0 + cw] * u_ref[h8 + 1:tm + h8 + 1, c0:c0 + cw])

    def block(fb):
        up(fb, u_ref.at[fb])
        down = None
        for n in range(n_chunks):
            c0 = n * cw
            a = conv(u_ref.at[fb], fb, c0)
            b = conv(u_ref.at[fb], fb, tf + c0)
            act = (a * jax.nn.sigmoid(a) * b).astype(BF16)
            d = jnp.dot(act, wd_ref[fb * tf + c0:fb * tf + c0 + cw, :], preferred_element_type=F32)
            down = d if down is None else down + d
        acc_ref[...] += down

    def blocks(count):
        for fb in range(count):
            block(fb)

    pl.when(s < n_full)(functools.partial(blocks, bps))
    if n_f % bps:
        pl.when(s == n_full)(functools.partial(blocks, n_f % bps))

    @pl.when(s == n_steps - 1)
    def _():
        def body(c, carry):
            r = pl.ds(pl.multiple_of(c * rows, rows), rows)
            y = x_ref[r, :] + acc_ref[r, :]
            o_ref[r, :] = _rms(y, fg_ref[...]) if final else y
            return carry
        lax.fori_loop(0, tm // rows, body, 0)


def _ffn(x1, ffn_norm, w_up_blk, conv_taps, w_down, final_norm, *, seq, final, tm=512, bps=2):
    T = x1.shape[0]
    h8 = SUBLANES_F32
    rb = tm // h8
    last_hb = T // h8 - 1
    n_f = w_up_blk.shape[0]
    tf = w_up_blk.shape[2] // 2
    return pl.pallas_call(
        functools.partial(_ffn_kernel, tm=tm, tf=tf, tiles_per_seq=seq // tm, n_f=n_f, bps=bps, final=final),
        grid=(T // tm, pl.cdiv(n_f, bps)),
        in_specs=[
            pl.BlockSpec((tm, D_MODEL), lambda i, s: (i, 0)),
            pl.BlockSpec((h8, D_MODEL), lambda i, s: (jnp.maximum(i * rb - 1, 0), 0)),
            pl.BlockSpec((h8, D_MODEL), lambda i, s: (jnp.minimum((i + 1) * rb, last_hb), 0)),
            pl.BlockSpec((1, D_MODEL), lambda i, s: (0, 0)),
            pl.BlockSpec((bps, D_MODEL, 2 * tf), lambda i, s: (s, 0, 0)),
            pl.BlockSpec((bps, 3, 2 * tf), lambda i, s: (s, 0, 0)),
            pl.BlockSpec((bps * tf, D_MODEL), lambda i, s: (s, 0)),
            pl.BlockSpec((1, D_MODEL), lambda i, s: (0, 0)),
        ],
        out_specs=pl.BlockSpec((tm, D_MODEL), lambda i, s: (i, 0)),
        out_shape=jax.ShapeDtypeStruct((T, D_MODEL), F32),
        scratch_shapes=[
            pltpu.VMEM((tm + 2 * h8, D_MODEL), BF16),
            pltpu.VMEM((bps, tm + 2 * h8, 2 * tf), F32),
        ],
        compiler_params=_params("parallel", "arbitrary"),
        name="ffn",
    )(x1, x1, x1, ffn_norm, w_up_blk, conv_taps, w_down, final_norm)


def _block_conv_taps(ffn_conv_w):
    k = ffn_conv_w.shape[0]
    n_f = D_FF // FFN_TF
    halves = ffn_conv_w.reshape(k, 2, n_f, FFN_TF)
    return jnp.transpose(halves, (2, 0, 1, 3)).reshape(n_f, k, 2 * FFN_TF)


def _pack_w_uq(w_uq):
    w = w_uq.reshape(Q_LORA, MLA_HEADS, QK_NOPE + QK_ROPE)
    nope = (w[:, :, :QK_NOPE] * Q_SCALE).reshape(Q_LORA, MLA_HEADS * QK_NOPE)
    rope = jnp.pad(w[:, :, QK_NOPE:], ((0, 0), (0, 0), (0, LANES - QK_ROPE))).reshape(Q_LORA, MLA_HEADS * LANES)
    return jnp.concatenate([nope, rope], axis=1).astype(BF16)


def kernel(x, mem, positions, mix_norm, w_in, q_norm, w_uq, kv_norm, w_ukv, w_o_mla, conv_w, w_out_conv,
           mem_norm, w_mem_kv, w_o_mem, gate_bias, w_o, ffn_norm, w_up, ffn_conv_w, w_down, final_norm):
    B, S, D = x.shape
    M = mem.shape[1]
    T = B * S
    depth = w_in.shape[0]
    x2 = x.reshape(T, D)
    mem2 = mem.reshape(B * M, D)
    pos = positions.reshape(T, 1)
    inv_freq = jnp.power(ROPE_THETA, -jnp.arange(0, QK_ROPE, 2, dtype=F32) / QK_ROPE)
    invf = jnp.concatenate([inv_freq, inv_freq, jnp.zeros((LANES - QK_ROPE,), F32)]).reshape(1, LANES)

    for l in range(depth):
        w_blk, w_kr = _pack_w_in(jnp.swapaxes(w_in[l], 0, 1))
        b_ukv, b_o_mla, b_out_conv, b_o_mem, b_mem_kv, b_o, b_down = _cast_bf16(
            w_ukv[l], w_o_mla[l], w_out_conv[l], w_o_mem[l], w_mem_kv[l], w_o[l], w_down[l])
        z, kr = _in_proj(x2, mix_norm[l].reshape(1, D), w_blk, w_kr)
        q, k, v = _qkv(z, kr, pos, invf, q_norm[l].reshape(1, -1), kv_norm[l].reshape(1, -1),
                       _pack_w_uq(w_uq[l]), b_ukv)
        o_mla = _mla(q, k, v, batch=B, seq=S)
        kvm = _mem_kv(mem2, mem_norm[l].reshape(1, D), b_mem_kv)
        yx = _mem_attn(z, kvm, seq=S, mem_len=M)
        merged = _merge(o_mla, z, yx, gate_bias[l].reshape(1, -1), conv_w[l], b_o_mla, b_out_conv, b_o_mem, seq=S)
        x2 = _out_proj(merged, b_o, x2)
        x2 = _ffn(x2, ffn_norm[l].reshape(1, D), _pack_w_up(w_up[l]), _block_conv_taps(ffn_conv_w[l]),
                  b_down, final_norm.reshape(1, D), seq=S, final=(l == depth - 1))
    return x2.reshape(B, S, D)
```

```python
import functools

import jax
import jax.numpy as jnp
from jax import lax
from jax.experimental import pallas as pl
from jax.experimental.pallas import tpu as pltpu

F32 = jnp.float32
BF16 = jnp.bfloat16

D_MODEL = 2048
MLA_HEADS = 8
Q_LORA = 512
KV_LORA = 512
QK_NOPE = 128
QK_ROPE = 64
V_HEAD = 128
ROPE_THETA = 10000.0
CONV_WIDTH = 1024
X_HEADS = 4
X_HEAD_DIM = 256
X_WIDTH = X_HEADS * X_HEAD_DIM
D_FF = 5632
N_BRANCH = 3
EPS = 1e-6
LOG2_E = 1.4426950408889634
Q_SCALE = (QK_NOPE + QK_ROPE) ** -0.5 * LOG2_E

LANES = 128
SUBLANES_F32 = 8
SUBLANES_BF16 = 16
HEAD_PAD = 256
VMEM_LIMIT = 56 * 1024 * 1024

Z_G = 0
Z_CQ = Z_G + N_BRANCH * D_MODEL
Z_CKV = Z_CQ + Q_LORA
Z_CV = Z_CKV + KV_LORA
Z_CB = Z_CV + CONV_WIDTH
Z_CC = Z_CB + CONV_WIDTH
Z_QX = Z_CC + CONV_WIDTH
Z_COLS = Z_QX + X_WIDTH


_NT = (((1,), (1,)), ((), ()))


def _params(*sem):
    return pltpu.CompilerParams(dimension_semantics=sem, vmem_limit_bytes=VMEM_LIMIT)


def _rms(x, g):
    inv = lax.rsqrt(jnp.mean(x * x, axis=-1, keepdims=True) + EPS)
    return x * inv * g


def _rms_rows_to(x_ref, g_ref, out_ref, rows, chunk, out_row0=0):
    def body(c, carry):
        r0 = pl.multiple_of(c * chunk, chunk)
        x = x_ref[pl.ds(r0, chunk), :].astype(F32)
        out_ref[pl.ds(out_row0 + r0, chunk), :] = _rms(x, g_ref[...]).astype(out_ref.dtype)
        return carry
    lax.fori_loop(0, rows // chunk, body, 0)


IN_TN = 1024
FFN_TF = 512


N_GATE_BLK = N_BRANCH * D_MODEL // IN_TN


def _pack_w_in_kernel(a_ref, b_ref, kr_ref, wblk_ref, wkr_ref):
    j = pl.program_id(0)

    @pl.when(j == N_GATE_BLK)
    def _():
        wblk_ref[0] = a_ref[...].astype(BF16)

    @pl.when(j != N_GATE_BLK)
    def _():
        wblk_ref[0, :IN_TN - QK_ROPE, :] = a_ref[QK_ROPE:, :].astype(BF16)
        wblk_ref[0, IN_TN - QK_ROPE:, :] = b_ref[...].astype(BF16)

    @pl.when(j == 0)
    def _():
        wkr_ref[:QK_ROPE, :] = kr_ref[...].astype(BF16)
        wkr_ref[QK_ROPE:, :] = jnp.zeros((LANES - QK_ROPE, wkr_ref.shape[1]), BF16)


def _pack_w_in(wi_t):
    cols, d = wi_t.shape
    kr0 = Q_LORA + KV_LORA
    assert cols == Z_COLS + QK_ROPE and kr0 == IN_TN and IN_TN % QK_ROPE == 0
    n_blk = Z_COLS // IN_TN
    n_tail = n_blk - 1 - N_GATE_BLK
    sub = IN_TN // QK_ROPE

    def src_blk(j):
        return jnp.where(j < N_GATE_BLK, j + 1 + n_tail, jnp.where(j == N_GATE_BLK, 0, j - N_GATE_BLK))

    return pl.pallas_call(
        _pack_w_in_kernel,
        grid=(n_blk,),
        in_specs=[
            pl.BlockSpec((IN_TN, d), lambda j: (src_blk(j), 0)),
            pl.BlockSpec((QK_ROPE, d), lambda j: ((src_blk(j) + 1) * sub, 0)),
            pl.BlockSpec((QK_ROPE, d), lambda j: (kr0 // QK_ROPE, 0)),
        ],
        out_specs=[
            pl.BlockSpec((1, IN_TN, d), lambda j: (j, 0, 0)),
            pl.BlockSpec((LANES, d), lambda j: (0, 0)),
        ],
        out_shape=[
            jax.ShapeDtypeStruct((n_blk, IN_TN, d), BF16),
            jax.ShapeDtypeStruct((LANES, d), BF16),
        ],
        compiler_params=_params("arbitrary"),
        name="pack_w_in",
    )(wi_t, wi_t, wi_t)


def _in_proj_kernel(x_ref, g_ref, w_ref, wkr_ref, wu_ref, wd_ref, z_ref, kr_ref, wub_ref, wdb_ref, h_ref, *, tm):
    @pl.when(pl.program_id(1) == 0)
    def _():
        _rms_rows_to(x_ref, g_ref, h_ref, tm, 128)
        kr_ref[...] = lax.dot_general(h_ref[...], wkr_ref[...], _NT, preferred_element_type=F32)

    z_ref[...] = lax.dot_general(h_ref[...], w_ref[0], _NT, preferred_element_type=F32).astype(BF16)
    wub_ref[0] = wu_ref[...].astype(BF16)
    wdb_ref[...] = wd_ref[...].astype(BF16)


def _in_proj(x2, g, w_blk, w_kr, w_up, w_down, *, tm=1024):
    T = x2.shape[0]
    tn = IN_TN
    n_i, n_j = T // tm, Z_COLS // tn
    n_steps = n_i * n_j
    d = w_up.shape[0]
    n_f = D_FF // FFN_TF
    up_rows = d * 2 * n_f // n_steps
    down_rows = w_down.shape[0] // n_steps
    assert up_rows * n_steps == d * 2 * n_f and d % up_rows == 0 and up_rows % SUBLANES_BF16 == 0
    assert down_rows * n_steps == w_down.shape[0] and down_rows % SUBLANES_BF16 == 0
    r_per_col = d // up_rows

    def up_in(i, j):
        step = i * n_j + j
        return (step % r_per_col, step // r_per_col)

    def up_out(i, j):
        step = i * n_j + j
        col = step // r_per_col
        return (col % n_f, step % r_per_col, col // n_f)

    return pl.pallas_call(
        functools.partial(_in_proj_kernel, tm=tm),
        grid=(n_i, n_j),
        in_specs=[
            pl.BlockSpec((tm, D_MODEL), lambda i, j: (i, 0)),
            pl.BlockSpec((1, D_MODEL), lambda i, j: (0, 0)),
            pl.BlockSpec((1, tn, D_MODEL), lambda i, j: (j, 0, 0)),
            pl.BlockSpec((LANES, D_MODEL), lambda i, j: (0, 0)),
            pl.BlockSpec((up_rows, FFN_TF), up_in),
            pl.BlockSpec((down_rows, D_MODEL), lambda i, j: (i * n_j + j, 0)),
        ],
        out_specs=[
            pl.BlockSpec((tm, tn), lambda i, j: (i, j)),
            pl.BlockSpec((tm, LANES), lambda i, j: (i, 0)),
            pl.BlockSpec((1, up_rows, FFN_TF), up_out),
            pl.BlockSpec((down_rows, D_MODEL), lambda i, j: (i * n_j + j, 0)),
        ],
        out_shape=[
            jax.ShapeDtypeStruct((T, Z_COLS), BF16),
            jax.ShapeDtypeStruct((T, LANES), F32),
            jax.ShapeDtypeStruct((n_f, d, 2 * FFN_TF), BF16),
            jax.ShapeDtypeStruct(w_down.shape, BF16),
        ],
        scratch_shapes=[pltpu.VMEM((tm, D_MODEL), BF16)],
        compiler_params=_params("arbitrary", "arbitrary"),
        name="in_proj",
    )(x2, g, w_blk, w_kr, w_up, w_down)


def _qkv_kernel(cq_ref, ckv_ref, kr_ref, pos_ref, invf_ref, qn_ref, kvn_ref, wuq_ref, wukv_ref,
                q_ref, k_ref, v_ref, *, tm):
    ang = pos_ref[...].astype(F32) * invf_ref[...]
    cos = jnp.cos(ang)
    sin = jnp.sin(ang)
    lane = lax.broadcasted_iota(jnp.int32, (tm, LANES), 1)
    half = QK_ROPE // 2
    c_tab = jnp.where(lane < QK_ROPE, cos, 0.0)
    s_lo = jnp.where(lane < half, -sin, 0.0)
    s_hi = jnp.where((lane >= half) & (lane < QK_ROPE), sin, 0.0)

    def rope(t, tabs):
        c, lo, hi = tabs
        return t * c + pltpu.roll(t, LANES - half, 1) * lo + pltpu.roll(t, half, 1) * hi

    k_tabs = (c_tab, s_lo, s_hi)
    q_tabs = tuple(t * Q_SCALE for t in k_tabs)
    cqn = _rms(cq_ref[...].astype(F32), qn_ref[...]).astype(BF16)
    q = jnp.dot(cqn, wuq_ref[...], preferred_element_type=F32)
    nope_cols = MLA_HEADS * QK_NOPE
    for h in range(MLA_HEADS):
        q_ref[:, h * HEAD_PAD:h * HEAD_PAD + QK_NOPE] = q[:, h * QK_NOPE:(h + 1) * QK_NOPE].astype(BF16)
        q_ref[:, h * HEAD_PAD + QK_NOPE:(h + 1) * HEAD_PAD] = (
            rope(q[:, nope_cols + h * LANES:nope_cols + (h + 1) * LANES], q_tabs)).astype(BF16)

    ckvn = _rms(ckv_ref[...].astype(F32), kvn_ref[...]).astype(BF16)
    kv = jnp.dot(ckvn, wukv_ref[...], preferred_element_type=F32)
    k_rope = rope(kr_ref[...], k_tabs).astype(BF16)
    ones = jnp.ones((tm, V_HEAD), BF16)
    for h in range(MLA_HEADS):
        k_ref[:, h * HEAD_PAD:h * HEAD_PAD + QK_NOPE] = kv[:, h * HEAD_PAD:h * HEAD_PAD + QK_NOPE].astype(BF16)
        k_ref[:, h * HEAD_PAD + QK_NOPE:(h + 1) * HEAD_PAD] = k_rope
        v_ref[:, 2 * h * V_HEAD:(2 * h + 1) * V_HEAD] = kv[:, h * HEAD_PAD + QK_NOPE:(h + 1) * HEAD_PAD].astype(BF16)
        v_ref[:, (2 * h + 1) * V_HEAD:(2 * h + 2) * V_HEAD] = ones


def _qkv(z, kr, pos, invf, q_norm, kv_norm, w_uq_p, w_ukv, *, tm=512):
    T = z.shape[0]
    qk_cols = MLA_HEADS * HEAD_PAD
    return pl.pallas_call(
        functools.partial(_qkv_kernel, tm=tm),
        grid=(T // tm,),
        in_specs=[
            pl.BlockSpec((tm, Q_LORA), lambda i: (i, Z_CQ // Q_LORA)),
            pl.BlockSpec((tm, KV_LORA), lambda i: (i, Z_CKV // KV_LORA)),
            pl.BlockSpec((tm, LANES), lambda i: (i, 0)),
            pl.BlockSpec((tm, 1), lambda i: (i, 0)),
            pl.BlockSpec((1, LANES), lambda i: (0, 0)),
            pl.BlockSpec((1, Q_LORA), lambda i: (0, 0)),
            pl.BlockSpec((1, KV_LORA), lambda i: (0, 0)),
            pl.BlockSpec((Q_LORA, qk_cols), lambda i: (0, 0)),
            pl.BlockSpec((KV_LORA, qk_cols), lambda i: (0, 0)),
        ],
        out_specs=[
            pl.BlockSpec((tm, qk_cols), lambda i: (i, 0)),
            pl.BlockSpec((tm, qk_cols), lambda i: (i, 0)),
            pl.BlockSpec((tm, 2 * MLA_HEADS * V_HEAD), lambda i: (i, 0)),
        ],
        out_shape=[
            jax.ShapeDtypeStruct((T, qk_cols), BF16),
            jax.ShapeDtypeStruct((T, qk_cols), BF16),
            jax.ShapeDtypeStruct((T, 2 * MLA_HEADS * V_HEAD), BF16),
        ],
        compiler_params=_params("parallel"),
        name="qkv",
    )(z, z, kr, pos, invf, q_norm, kv_norm, w_uq_p, w_ukv)


def _mla_kernel(q_ref, k_ref, v_ref, o_ref, *, seq, tq, heads):
    for h in range(heads):
        k = k_ref[:, h * HEAD_PAD:(h + 1) * HEAD_PAD]
        v = v_ref[:, 2 * h * V_HEAD:2 * (h + 1) * V_HEAD]
        for c in range(seq // tq):
            r = slice(c * tq, (c + 1) * tq)
            s = lax.dot_general(q_ref[r, h * HEAD_PAD:(h + 1) * HEAD_PAD], k, (((1,), (1,)), ((), ())),
                                preferred_element_type=F32)
            m = jnp.max(s, axis=-1, keepdims=True)
            p = jnp.exp2(s - m).astype(BF16)
            ov = jnp.dot(p, v, preferred_element_type=F32)
            o_ref[r, h * V_HEAD:(h + 1) * V_HEAD] = (ov[:, :V_HEAD] / ov[:, V_HEAD:]).astype(BF16)


def _mla(q, k, v, *, batch, seq, tq=512, heads=4):
    T = q.shape[0]
    return pl.pallas_call(
        functools.partial(_mla_kernel, seq=seq, tq=tq, heads=heads),
        grid=(batch, MLA_HEADS // heads),
        in_specs=[
            pl.BlockSpec((seq, heads * HEAD_PAD), lambda b, h: (b, h)),
            pl.BlockSpec((seq, heads * HEAD_PAD), lambda b, h: (b, h)),
            pl.BlockSpec((seq, heads * 2 * V_HEAD), lambda b, h: (b, h)),
        ],
        out_specs=pl.BlockSpec((seq, heads * V_HEAD), lambda b, h: (b, h)),
        out_shape=jax.ShapeDtypeStruct((T, MLA_HEADS * V_HEAD), BF16),
        compiler_params=_params("parallel", "parallel"),
        name="mla_attn",
    )(q, k, v)


def _mem_kv_kernel(m_ref, g_ref, w_ref, o_ref, h_ref, *, tm):
    @pl.when(pl.program_id(1) == 0)
    def _():
        _rms_rows_to(m_ref, g_ref, h_ref, tm, 128)

    o_ref[...] = jnp.dot(h_ref[...], w_ref[...], preferred_element_type=F32).astype(BF16)


def _mem_kv(mem2, g, w, *, tm=1024, tn=1024):
    R = mem2.shape[0]
    N = w.shape[1]
    return pl.pallas_call(
        functools.partial(_mem_kv_kernel, tm=tm),
        grid=(R // tm, N // tn),
        in_specs=[
            pl.BlockSpec((tm, D_MODEL), lambda i, j: (i, 0)),
            pl.BlockSpec((1, D_MODEL), lambda i, j: (0, 0)),
            pl.BlockSpec((D_MODEL, tn), lambda i, j: (0, j)),
        ],
        out_specs=pl.BlockSpec((tm, tn), lambda i, j: (i, j)),
        out_shape=jax.ShapeDtypeStruct((R, N), BF16),
        scratch_shapes=[pltpu.VMEM((tm, D_MODEL), BF16)],
        compiler_params=_params("parallel", "arbitrary"),
        name="mem_kv",
    )(mem2, g, w)


def _mem_attn_kernel(q_ref, kv_ref, o_ref):
    scale = X_HEAD_DIM ** -0.5
    for h in range(X_HEADS):
        c = slice(h * X_HEAD_DIM, (h + 1) * X_HEAD_DIM)
        k = kv_ref[:, h * X_HEAD_DIM:(h + 1) * X_HEAD_DIM]
        v = kv_ref[:, X_WIDTH + h * X_HEAD_DIM:X_WIDTH + (h + 1) * X_HEAD_DIM]
        s = lax.dot_general(q_ref[:, c], k, (((1,), (1,)), ((), ())), preferred_element_type=F32) * scale
        m = jnp.max(s, axis=-1, keepdims=True)
        p = jnp.exp(s - m)
        l = jnp.sum(p, axis=-1, keepdims=True)
        o = jnp.dot(p.astype(BF16), v, preferred_element_type=F32)
        o_ref[:, c] = (o / l).astype(BF16)


def _mem_attn(z, kvm, *, seq, mem_len, tm=1024):
    T = z.shape[0]
    return pl.pallas_call(
        _mem_attn_kernel,
        grid=(T // tm,),
        in_specs=[
            pl.BlockSpec((tm, X_WIDTH), lambda i: (i, Z_QX // X_WIDTH)),
            pl.BlockSpec((mem_len, 2 * X_WIDTH), lambda i: (i // (seq // tm), 0)),
        ],
        out_specs=pl.BlockSpec((tm, X_WIDTH), lambda i: (i, 0)),
        out_shape=jax.ShapeDtypeStruct((T, X_WIDTH), BF16),
        compiler_params=_params("parallel"),
        name="mem_attn",
    )(z, kvm)


def _merge_kernel(o_ref, cv_ref, cb_ref, cc_ref, cvp_ref, ccp_ref, cvn_ref, ccn_ref, yx_ref,
                  g0_ref, g1_ref, g2_ref, b_ref, cw_ref,
                  wa_ref, wc_ref, wm_ref, out_ref, pext_ref, *, tm, tn, tiles_per_seq):
    pos_in_seq = pl.program_id(0) % tiles_per_seq
    keep_prev = jnp.where(pos_in_seq == 0, 0.0, 1.0)
    keep_next = jnp.where(pos_in_seq == tiles_per_seq - 1, 0.0, 1.0)
    h8 = SUBLANES_F32
    prev = (cvp_ref[h8:, :].astype(F32) * ccp_ref[h8:, :].astype(F32)) * keep_prev
    nxt = (cvn_ref[:h8, :].astype(F32) * ccn_ref[:h8, :].astype(F32)) * keep_next
    pext_ref[0:h8, :] = prev
    pext_ref[tm + h8:tm + 2 * h8, :] = nxt
    pext_ref[h8:tm + h8, :] = cv_ref[...].astype(F32) * cc_ref[...].astype(F32)
    conv = (cw_ref[0:1, :] * pext_ref[h8 - 1:tm + h8 - 1, :] + cw_ref[1:2, :] * pext_ref[h8:tm + h8, :]
            + cw_ref[2:3, :] * pext_ref[h8 + 1:tm + h8 + 1, :])
    cbr = (cb_ref[...].astype(F32) * conv).astype(BF16)

    def gate(g_ref, k, c):
        return jax.nn.sigmoid(g_ref[:, c].astype(F32) + b_ref[:, k * D_MODEL + c.start:k * D_MODEL + c.stop])

    for n in range(D_MODEL // tn):
        c = slice(n * tn, (n + 1) * tn)
        y = gate(g0_ref, 0, c) * jnp.dot(o_ref[...], wa_ref[:, c], preferred_element_type=F32)
        y += gate(g1_ref, 1, c) * jnp.dot(cbr, wc_ref[:, c], preferred_element_type=F32)
        y += gate(g2_ref, 2, c) * jnp.dot(yx_ref[...], wm_ref[:, c], preferred_element_type=F32)
        out_ref[:, c] = y.astype(BF16)


def _merge(o_mla, z, yx, gate_bias, conv_w, w_o_mla, w_out_conv, w_o_mem, *, seq, tm=512, tn=2048):
    T = z.shape[0]
    hb = SUBLANES_BF16
    rb = tm // hb
    last_hb = T // hb - 1
    cw = CONV_WIDTH
    once = pl.Buffered(1)

    def zcol(c0, width):
        return c0 // width

    def gate_spec(k):
        return pl.BlockSpec((tm, D_MODEL), lambda i, k=k: (i, Z_G // D_MODEL + k))

    def prev_spec(c0):
        return pl.BlockSpec((hb, cw), lambda i: (jnp.maximum(i * rb - 1, 0), zcol(c0, cw)))

    def next_spec(c0):
        return pl.BlockSpec((hb, cw), lambda i: (jnp.minimum((i + 1) * rb, last_hb), zcol(c0, cw)))

    def w_spec():
        return pl.BlockSpec((cw, D_MODEL), lambda i: (0, 0), pipeline_mode=once)

    assert Z_G % D_MODEL == 0
    return pl.pallas_call(
        functools.partial(_merge_kernel, tm=tm, tn=tn, tiles_per_seq=seq // tm),
        grid=(T // tm,),
        in_specs=[
            pl.BlockSpec((tm, MLA_HEADS * V_HEAD), lambda i: (i, 0)),
            pl.BlockSpec((tm, cw), lambda i: (i, zcol(Z_CV, cw))),
            pl.BlockSpec((tm, cw), lambda i: (i, zcol(Z_CB, cw))),
            pl.BlockSpec((tm, cw), lambda i: (i, zcol(Z_CC, cw))),
            prev_spec(Z_CV), prev_spec(Z_CC), next_spec(Z_CV), next_spec(Z_CC),
            pl.BlockSpec((tm, X_WIDTH), lambda i: (i, 0)),
            gate_spec(0), gate_spec(1), gate_spec(2),
            pl.BlockSpec((1, N_BRANCH * D_MODEL), lambda i: (0, 0), pipeline_mode=once),
            pl.BlockSpec((3, cw), lambda i: (0, 0), pipeline_mode=once),
            w_spec(), w_spec(), w_spec(),
        ],
        out_specs=pl.BlockSpec((tm, D_MODEL), lambda i: (i, 0)),
        out_shape=jax.ShapeDtypeStruct((T, D_MODEL), BF16),
        scratch_shapes=[pltpu.VMEM((tm + 2 * SUBLANES_F32, cw), F32)],
        compiler_params=_params("parallel"),
        name="merge",
    )(o_mla, z, z, z, z, z, z, z, yx, z, z, z, gate_bias, conv_w, w_o_mla, w_out_conv, w_o_mem)


def _out_proj_kernel(m_ref, w_ref, x_ref, o_ref):
    o_ref[...] = x_ref[...] + jnp.dot(m_ref[...], w_ref[...], preferred_element_type=F32)


def _out_proj(merged, w_o, x2, *, tm=512):
    T = x2.shape[0]
    return pl.pallas_call(
        _out_proj_kernel,
        grid=(T // tm,),
        in_specs=[
            pl.BlockSpec((tm, D_MODEL), lambda i: (i, 0)),
            pl.BlockSpec((D_MODEL, D_MODEL), lambda i: (0, 0), pipeline_mode=pl.Buffered(1)),
            pl.BlockSpec((tm, D_MODEL), lambda i: (i, 0)),
        ],
        out_specs=pl.BlockSpec((tm, D_MODEL), lambda i: (i, 0)),
        out_shape=jax.ShapeDtypeStruct((T, D_MODEL), F32),
        compiler_params=_params("parallel"),
        name="out_proj",
    )(merged, w_o, x2)


FFN_COL_CHUNK = 256


def _ffn_kernel(x_ref, xp_ref, xn_ref, g_ref, wu_ref, ct_ref, wd_ref, fg_ref,
                o_ref, hn_ref, u_ref, *, tm, tf, tiles_per_seq, n_f, bps, final):
    acc_ref = o_ref
    i = pl.program_id(0)
    s = pl.program_id(1)
    h8 = SUBLANES_F32
    cw = FFN_COL_CHUNK
    rows = 128
    n_chunks = tf // cw
    n_full = n_f // bps
    n_steps = pl.cdiv(n_f, bps)

    @pl.when(s == 0)
    def _():
        pos_in_seq = i % tiles_per_seq
        keep_prev = jnp.where(pos_in_seq == 0, 0.0, 1.0)
        keep_next = jnp.where(pos_in_seq == tiles_per_seq - 1, 0.0, 1.0)
        halo = jnp.concatenate([_rms(xp_ref[...], g_ref[...]) * keep_prev,
                                _rms(xn_ref[...], g_ref[...]) * keep_next], axis=0)
        hn_ref[tm:tm + 2 * h8, :] = halo.astype(BF16)
        _rms_rows_to(x_ref, g_ref, hn_ref, tm, rows)
        acc_ref[...] = jnp.zeros_like(acc_ref)

    def up(fb, u_ref):
        r = jnp.dot(hn_ref[...], wu_ref[fb], preferred_element_type=F32)
        u_ref[h8:tm + h8, :] = r[:tm]
        u_ref[0:h8, :] = r[tm:tm + h8]
        u_ref[tm + h8:tm + 2 * h8, :] = r[tm + h8:]

    def conv(u_ref, fb, c0):
        return (ct_ref[fb, 0:1, c0:c0 + cw] * u_ref[h8 - 1:tm + h8 - 1, c0:c0 + cw]
                + ct_ref[fb, 1:2, c0:c0 + cw] * u_ref[h8:tm + h8, c0:c0 + cw]
                + ct_ref[fb, 2:3, c0:c0 + cw] * u_ref[h8 + 1:tm + h8 + 1, c0:c0 + cw])

    def block(fb):
        up(fb, u_ref.at[fb])
        down = None
        for n in range(n_chunks):
            c0 = n * cw
            a = conv(u_ref.at[fb], fb, c0)
            b = conv(u_ref.at[fb], fb, tf + c0)
            act = (a * jax.nn.sigmoid(a) * b).astype(BF16)
            d = jnp.dot(act, wd_ref[fb * tf + c0:fb * tf + c0 + cw, :], preferred_element_type=F32)
            down = d if down is None else down + d
        acc_ref[...] += down

    def blocks(count):
        for fb in range(count):
            block(fb)

    pl.when(s < n_full)(functools.partial(blocks, bps))
    if n_f % bps:
        pl.when(s == n_full)(functools.partial(blocks, n_f % bps))

    @pl.when(s == n_steps - 1)
    def _():
        def body(c, carry):
            r = pl.ds(pl.multiple_of(c * rows, rows), rows)
            y = x_ref[r, :] + acc_ref[r, :]
            o_ref[r, :] = _rms(y, fg_ref[...]) if final else y
            return carry
        lax.fori_loop(0, tm // rows, body, 0)


def _ffn(x1, ffn_norm, w_up_blk, conv_taps, w_down, final_norm, *, seq, final, tm=512, bps=2):
    T = x1.shape[0]
    h8 = SUBLANES_F32
    rb = tm // h8
    last_hb = T // h8 - 1
    n_f = w_up_blk.shape[0]
    tf = w_up_blk.shape[2] // 2
    return pl.pallas_call(
        functools.partial(_ffn_kernel, tm=tm, tf=tf, tiles_per_seq=seq // tm, n_f=n_f, bps=bps, final=final),
        grid=(T // tm, pl.cdiv(n_f, bps)),
        in_specs=[
            pl.BlockSpec((tm, D_MODEL), lambda i, s: (i, 0)),
            pl.BlockSpec((h8, D_MODEL), lambda i, s: (jnp.maximum(i * rb - 1, 0), 0)),
            pl.BlockSpec((h8, D_MODEL), lambda i, s: (jnp.minimum((i + 1) * rb, last_hb), 0)),
            pl.BlockSpec((1, D_MODEL), lambda i, s: (0, 0)),
            pl.BlockSpec((bps, D_MODEL, 2 * tf), lambda i, s: (s, 0, 0)),
            pl.BlockSpec((bps, 3, 2 * tf), lambda i, s: (s, 0, 0)),
            pl.BlockSpec((bps * tf, D_MODEL), lambda i, s: (s, 0)),
            pl.BlockSpec((1, D_MODEL), lambda i, s: (0, 0)),
        ],
        out_specs=pl.BlockSpec((tm, D_MODEL), lambda i, s: (i, 0)),
        out_shape=jax.ShapeDtypeStruct((T, D_MODEL), F32),
        scratch_shapes=[
            pltpu.VMEM((tm + 2 * h8, D_MODEL), BF16),
            pltpu.VMEM((bps, tm + 2 * h8, 2 * tf), F32),
        ],
        compiler_params=_params("parallel", "arbitrary"),
        name="ffn",
    )(x1, x1, x1, ffn_norm, w_up_blk, conv_taps, w_down, final_norm)


def _block_conv_taps(ffn_conv_w):
    k = ffn_conv_w.shape[0]
    n_f = D_FF // FFN_TF
    halves = ffn_conv_w.reshape(k, 2, n_f, FFN_TF)
    return jnp.transpose(halves, (2, 0, 1, 3)).reshape(n_f, k, 2 * FFN_TF)


def _pack_w_uq(w_uq):
    w = w_uq.reshape(Q_LORA, MLA_HEADS, QK_NOPE + QK_ROPE)
    nope = (w[:, :, :QK_NOPE] * Q_SCALE).reshape(Q_LORA, MLA_HEADS * QK_NOPE)
    rope = jnp.pad(w[:, :, QK_NOPE:], ((0, 0), (0, 0), (0, LANES - QK_ROPE))).reshape(Q_LORA, MLA_HEADS * LANES)
    return jnp.concatenate([nope, rope], axis=1).astype(BF16)


def kernel(x, mem, positions, mix_norm, w_in, q_norm, w_uq, kv_norm, w_ukv, w_o_mla, conv_w, w_out_conv,
           mem_norm, w_mem_kv, w_o_mem, gate_bias, w_o, ffn_norm, w_up, ffn_conv_w, w_down, final_norm):
    B, S, D = x.shape
    M = mem.shape[1]
    T = B * S
    depth = w_in.shape[0]
    x2 = x.reshape(T, D)
    mem2 = mem.reshape(B * M, D)
    pos = positions.reshape(T, 1)
    inv_freq = jnp.power(ROPE_THETA, -jnp.arange(0, QK_ROPE, 2, dtype=F32) / QK_ROPE)
    invf = jnp.concatenate([inv_freq, inv_freq, jnp.zeros((LANES - QK_ROPE,), F32)]).reshape(1, LANES)

    for l in range(depth):
        w_blk, w_kr = _pack_w_in(jnp.swapaxes(w_in[l], 0, 1))
        z, kr, w_up_blk, b_down = _in_proj(x2, mix_norm[l].reshape(1, D), w_blk, w_kr, w_up[l], w_down[l])
        q, k, v = _qkv(z, kr, pos, invf, q_norm[l].reshape(1, -1), kv_norm[l].reshape(1, -1),
                       _pack_w_uq(w_uq[l]), w_ukv[l].astype(BF16))
        o_mla = _mla(q, k, v, batch=B, seq=S)
        kvm = _mem_kv(mem2, mem_norm[l].reshape(1, D), w_mem_kv[l].astype(BF16))
        yx = _mem_attn(z, kvm, seq=S, mem_len=M)
        merged = _merge(o_mla, z, yx, gate_bias[l].reshape(1, -1), conv_w[l],
                        w_o_mla[l].astype(BF16), w_out_conv[l].astype(BF16), w_o_mem[l].astype(BF16), seq=S)
        x2 = _out_proj(merged, w_o[l].astype(BF16), x2)
        x2 = _ffn(x2, ffn_norm[l].reshape(1, D), w_up_blk, _block_conv_taps(ffn_conv_w[l]),
                  b_down, final_norm.reshape(1, D), seq=S, final=(l == depth - 1))
    return x2.reshape(B, S, D)
```

```python
import functools

import jax
import jax.numpy as jnp
from jax import lax
from jax.experimental import pallas as pl
from jax.experimental.pallas import tpu as pltpu

F32 = jnp.float32
BF16 = jnp.bfloat16

D_MODEL = 2048
MLA_HEADS = 8
Q_LORA = 512
KV_LORA = 512
QK_NOPE = 128
QK_ROPE = 64
V_HEAD = 128
ROPE_THETA = 10000.0
CONV_WIDTH = 1024
X_HEADS = 4
X_HEAD_DIM = 256
X_WIDTH = X_HEADS * X_HEAD_DIM
D_FF = 5632
N_BRANCH = 3
EPS = 1e-6
LOG2_E = 1.4426950408889634
Q_SCALE = (QK_NOPE + QK_ROPE) ** -0.5 * LOG2_E

LANES = 128
SUBLANES_F32 = 8
SUBLANES_BF16 = 16
HEAD_PAD = 256
VMEM_LIMIT = 56 * 1024 * 1024

Z_G = 0
Z_CQ = Z_G + N_BRANCH * D_MODEL
Z_CKV = Z_CQ + Q_LORA
Z_CV = Z_CKV + KV_LORA
Z_CB = Z_CV + CONV_WIDTH
Z_CC = Z_CB + CONV_WIDTH
Z_QX = Z_CC + CONV_WIDTH
Z_COLS = Z_QX + X_WIDTH


_NT = (((1,), (1,)), ((), ()))


def _params(*sem):
    return pltpu.CompilerParams(dimension_semantics=sem, vmem_limit_bytes=VMEM_LIMIT)


def _rms(x, g):
    inv = lax.rsqrt(jnp.mean(x * x, axis=-1, keepdims=True) + EPS)
    return x * inv * g


def _rms_rows_to(x_ref, g_ref, out_ref, rows, chunk, out_row0=0):
    def body(c, carry):
        r0 = pl.multiple_of(c * chunk, chunk)
        x = x_ref[pl.ds(r0, chunk), :].astype(F32)
        out_ref[pl.ds(out_row0 + r0, chunk), :] = _rms(x, g_ref[...]).astype(out_ref.dtype)
        return carry
    lax.fori_loop(0, rows // chunk, body, 0)


IN_TN = 1024
FFN_TF = 512


N_GATE_BLK = N_BRANCH * D_MODEL // IN_TN


def _pack_w_in_kernel(a_ref, b_ref, kr_ref, wblk_ref, wkr_ref):
    j = pl.program_id(0)

    @pl.when(j == N_GATE_BLK)
    def _():
        wblk_ref[0] = a_ref[...].astype(BF16)

    @pl.when(j != N_GATE_BLK)
    def _():
        wblk_ref[0, :IN_TN - QK_ROPE, :] = a_ref[QK_ROPE:, :].astype(BF16)
        wblk_ref[0, IN_TN - QK_ROPE:, :] = b_ref[...].astype(BF16)

    @pl.when(j == 0)
    def _():
        wkr_ref[:QK_ROPE, :] = kr_ref[...].astype(BF16)
        wkr_ref[QK_ROPE:, :] = jnp.zeros((LANES - QK_ROPE, wkr_ref.shape[1]), BF16)


def _pack_w_in(wi_t):
    cols, d = wi_t.shape
    kr0 = Q_LORA + KV_LORA
    assert cols == Z_COLS + QK_ROPE and kr0 == IN_TN and IN_TN % QK_ROPE == 0
    n_blk = Z_COLS // IN_TN
    n_tail = n_blk - 1 - N_GATE_BLK
    sub = IN_TN // QK_ROPE

    def src_blk(j):
        return jnp.where(j < N_GATE_BLK, j + 1 + n_tail, jnp.where(j == N_GATE_BLK, 0, j - N_GATE_BLK))

    return pl.pallas_call(
        _pack_w_in_kernel,
        grid=(n_blk,),
        in_specs=[
            pl.BlockSpec((IN_TN, d), lambda j: (src_blk(j), 0)),
            pl.BlockSpec((QK_ROPE, d), lambda j: ((src_blk(j) + 1) * sub, 0)),
            pl.BlockSpec((QK_ROPE, d), lambda j: (kr0 // QK_ROPE, 0)),
        ],
        out_specs=[
            pl.BlockSpec((1, IN_TN, d), lambda j: (j, 0, 0)),
            pl.BlockSpec((LANES, d), lambda j: (0, 0)),
        ],
        out_shape=[
            jax.ShapeDtypeStruct((n_blk, IN_TN, d), BF16),
            jax.ShapeDtypeStruct((LANES, d), BF16),
        ],
        compiler_params=_params("arbitrary"),
        name="pack_w_in",
    )(wi_t, wi_t, wi_t)


def _in_proj_kernel(x_ref, g_ref, w_ref, wkr_ref, wu_ref, wd_ref, z_ref, kr_ref, wub_ref, wdb_ref, h_ref, *, tm):
    @pl.when(pl.program_id(1) == 0)
    def _():
        _rms_rows_to(x_ref, g_ref, h_ref, tm, 128)
        kr_ref[...] = lax.dot_general(h_ref[...], wkr_ref[...], _NT, preferred_element_type=F32)

    z_ref[...] = lax.dot_general(h_ref[...], w_ref[0], _NT, preferred_element_type=F32).astype(BF16)
    wub_ref[0] = wu_ref[...].astype(BF16)
    wdb_ref[...] = wd_ref[...].astype(BF16)


def _in_proj(x2, g, w_blk, w_kr, w_up, w_down, *, tm=1024):
    T = x2.shape[0]
    tn = IN_TN
    n_i, n_j = T // tm, Z_COLS // tn
    n_steps = n_i * n_j
    d = w_up.shape[0]
    n_f = D_FF // FFN_TF
    up_rows = d * 2 * n_f // n_steps
    down_rows = w_down.shape[0] // n_steps
    assert up_rows * n_steps == d * 2 * n_f and d % up_rows == 0 and up_rows % SUBLANES_BF16 == 0
    assert down_rows * n_steps == w_down.shape[0] and down_rows % SUBLANES_BF16 == 0
    r_per_col = d // up_rows

    def up_in(i, j):
        step = i * n_j + j
        return (step % r_per_col, step // r_per_col)

    def up_out(i, j):
        step = i * n_j + j
        col = step // r_per_col
        return (col % n_f, step % r_per_col, col // n_f)

    return pl.pallas_call(
        functools.partial(_in_proj_kernel, tm=tm),
        grid=(n_i, n_j),
        in_specs=[
            pl.BlockSpec((tm, D_MODEL), lambda i, j: (i, 0)),
            pl.BlockSpec((1, D_MODEL), lambda i, j: (0, 0)),
            pl.BlockSpec((1, tn, D_MODEL), lambda i, j: (j, 0, 0)),
            pl.BlockSpec((LANES, D_MODEL), lambda i, j: (0, 0)),
            pl.BlockSpec((up_rows, FFN_TF), up_in),
            pl.BlockSpec((down_rows, D_MODEL), lambda i, j: (i * n_j + j, 0)),
        ],
        out_specs=[
            pl.BlockSpec((tm, tn), lambda i, j: (i, j)),
            pl.BlockSpec((tm, LANES), lambda i, j: (i, 0)),
            pl.BlockSpec((1, up_rows, FFN_TF), up_out),
            pl.BlockSpec((down_rows, D_MODEL), lambda i, j: (i * n_j + j, 0)),
        ],
        out_shape=[
            jax.ShapeDtypeStruct((T, Z_COLS), BF16),
            jax.ShapeDtypeStruct((T, LANES), F32),
            jax.ShapeDtypeStruct((n_f, d, 2 * FFN_TF), BF16),
            jax.ShapeDtypeStruct(w_down.shape, BF16),
        ],
        scratch_shapes=[pltpu.VMEM((tm, D_MODEL), BF16)],
        compiler_params=_params("arbitrary", "arbitrary"),
        name="in_proj",
    )(x2, g, w_blk, w_kr, w_up, w_down)


def _qkv_kernel(cq_ref, ckv_ref, kr_ref, pos_ref, invf_ref, qn_ref, kvn_ref, wuq_ref, wukv_ref, *rest, tm):
    n_w = (len(rest) - 3) // 2
    q_ref, k_ref, v_ref = rest[n_w:n_w + 3]
    for w_ref, b_ref in zip(rest[:n_w], rest[n_w + 3:]):
        b_ref[...] = w_ref[...].astype(BF16)

    ang = pos_ref[...].astype(F32) * invf_ref[...]
    cos = jnp.cos(ang)
    sin = jnp.sin(ang)
    lane = lax.broadcasted_iota(jnp.int32, (tm, LANES), 1)
    half = QK_ROPE // 2
    c_tab = jnp.where(lane < QK_ROPE, cos, 0.0)
    s_lo = jnp.where(lane < half, -sin, 0.0)
    s_hi = jnp.where((lane >= half) & (lane < QK_ROPE), sin, 0.0)

    def rope(t, tabs):
        c, lo, hi = tabs
        return t * c + pltpu.roll(t, LANES - half, 1) * lo + pltpu.roll(t, half, 1) * hi

    k_tabs = (c_tab, s_lo, s_hi)
    q_tabs = tuple(t * Q_SCALE for t in k_tabs)
    cqn = _rms(cq_ref[...].astype(F32), qn_ref[...]).astype(BF16)
    q = jnp.dot(cqn, wuq_ref[...], preferred_element_type=F32)
    nope_cols = MLA_HEADS * QK_NOPE
    for h in range(MLA_HEADS):
        q_ref[:, h * HEAD_PAD:h * HEAD_PAD + QK_NOPE] = q[:, h * QK_NOPE:(h + 1) * QK_NOPE].astype(BF16)
        q_ref[:, h * HEAD_PAD + QK_NOPE:(h + 1) * HEAD_PAD] = (
            rope(q[:, nope_cols + h * LANES:nope_cols + (h + 1) * LANES], q_tabs)).astype(BF16)

    ckvn = _rms(ckv_ref[...].astype(F32), kvn_ref[...]).astype(BF16)
    kv = jnp.dot(ckvn, wukv_ref[...], preferred_element_type=F32)
    k_rope = rope(kr_ref[...], k_tabs).astype(BF16)
    ones = jnp.ones((tm, V_HEAD), BF16)
    for h in range(MLA_HEADS):
        k_ref[:, h * HEAD_PAD:h * HEAD_PAD + QK_NOPE] = kv[:, h * HEAD_PAD:h * HEAD_PAD + QK_NOPE].astype(BF16)
        k_ref[:, h * HEAD_PAD + QK_NOPE:(h + 1) * HEAD_PAD] = k_rope
        v_ref[:, 2 * h * V_HEAD:(2 * h + 1) * V_HEAD] = kv[:, h * HEAD_PAD + QK_NOPE:(h + 1) * HEAD_PAD].astype(BF16)
        v_ref[:, (2 * h + 1) * V_HEAD:(2 * h + 2) * V_HEAD] = ones


def _qkv(z, kr, pos, invf, q_norm, kv_norm, w_uq_p, w_ukv, later_ws, *, tm=512):
    T = z.shape[0]
    qk_cols = MLA_HEADS * HEAD_PAD
    n_steps = T // tm
    assert all(w.shape[0] % (n_steps * SUBLANES_BF16) == 0 for w in later_ws)

    def slice_spec(w):
        return pl.BlockSpec((w.shape[0] // n_steps, w.shape[1]), lambda i: (i, 0))

    outs = pl.pallas_call(
        functools.partial(_qkv_kernel, tm=tm),
        grid=(n_steps,),
        in_specs=[
            pl.BlockSpec((tm, Q_LORA), lambda i: (i, Z_CQ // Q_LORA)),
            pl.BlockSpec((tm, KV_LORA), lambda i: (i, Z_CKV // KV_LORA)),
            pl.BlockSpec((tm, LANES), lambda i: (i, 0)),
            pl.BlockSpec((tm, 1), lambda i: (i, 0)),
            pl.BlockSpec((1, LANES), lambda i: (0, 0)),
            pl.BlockSpec((1, Q_LORA), lambda i: (0, 0)),
            pl.BlockSpec((1, KV_LORA), lambda i: (0, 0)),
            pl.BlockSpec((Q_LORA, qk_cols), lambda i: (0, 0)),
            pl.BlockSpec((KV_LORA, qk_cols), lambda i: (0, 0)),
        ] + [slice_spec(w) for w in later_ws],
        out_specs=[
            pl.BlockSpec((tm, qk_cols), lambda i: (i, 0)),
            pl.BlockSpec((tm, qk_cols), lambda i: (i, 0)),
            pl.BlockSpec((tm, 2 * MLA_HEADS * V_HEAD), lambda i: (i, 0)),
        ] + [slice_spec(w) for w in later_ws],
        out_shape=[
            jax.ShapeDtypeStruct((T, qk_cols), BF16),
            jax.ShapeDtypeStruct((T, qk_cols), BF16),
            jax.ShapeDtypeStruct((T, 2 * MLA_HEADS * V_HEAD), BF16),
        ] + [jax.ShapeDtypeStruct(w.shape, BF16) for w in later_ws],
        compiler_params=_params("arbitrary"),
        name="qkv",
    )(z, z, kr, pos, invf, q_norm, kv_norm, w_uq_p, w_ukv, *later_ws)
    return outs[0], outs[1], outs[2], outs[3:]


def _mla_kernel(q_ref, k_ref, v_ref, o_ref, *, seq, tq, heads):
    for h in range(heads):
        k = k_ref[:, h * HEAD_PAD:(h + 1) * HEAD_PAD]
        v = v_ref[:, 2 * h * V_HEAD:2 * (h + 1) * V_HEAD]
        for c in range(seq // tq):
            r = slice(c * tq, (c + 1) * tq)
            s = lax.dot_general(q_ref[r, h * HEAD_PAD:(h + 1) * HEAD_PAD], k, (((1,), (1,)), ((), ())),
                                preferred_element_type=F32)
            m = jnp.max(s, axis=-1, keepdims=True)
            p = jnp.exp2(s - m).astype(BF16)
            ov = jnp.dot(p, v, preferred_element_type=F32)
            o_ref[r, h * V_HEAD:(h + 1) * V_HEAD] = (ov[:, :V_HEAD] / ov[:, V_HEAD:]).astype(BF16)


def _mla(q, k, v, *, batch, seq, tq=512, heads=4):
    T = q.shape[0]
    return pl.pallas_call(
        functools.partial(_mla_kernel, seq=seq, tq=tq, heads=heads),
        grid=(batch, MLA_HEADS // heads),
        in_specs=[
            pl.BlockSpec((seq, heads * HEAD_PAD), lambda b, h: (b, h)),
            pl.BlockSpec((seq, heads * HEAD_PAD), lambda b, h: (b, h)),
            pl.BlockSpec((seq, heads * 2 * V_HEAD), lambda b, h: (b, h)),
        ],
        out_specs=pl.BlockSpec((seq, heads * V_HEAD), lambda b, h: (b, h)),
        out_shape=jax.ShapeDtypeStruct((T, MLA_HEADS * V_HEAD), BF16),
        compiler_params=_params("parallel", "parallel"),
        name="mla_attn",
    )(q, k, v)


def _mem_kv_kernel(m_ref, g_ref, w_ref, o_ref, h_ref, *, tm):
    @pl.when(pl.program_id(1) == 0)
    def _():
        _rms_rows_to(m_ref, g_ref, h_ref, tm, 128)

    o_ref[...] = jnp.dot(h_ref[...], w_ref[...], preferred_element_type=F32).astype(BF16)


def _mem_kv(mem2, g, w, *, tm=1024, tn=1024):
    R = mem2.shape[0]
    N = w.shape[1]
    return pl.pallas_call(
        functools.partial(_mem_kv_kernel, tm=tm),
        grid=(R // tm, N // tn),
        in_specs=[
            pl.BlockSpec((tm, D_MODEL), lambda i, j: (i, 0)),
            pl.BlockSpec((1, D_MODEL), lambda i, j: (0, 0)),
            pl.BlockSpec((D_MODEL, tn), lambda i, j: (0, j)),
        ],
        out_specs=pl.BlockSpec((tm, tn), lambda i, j: (i, j)),
        out_shape=jax.ShapeDtypeStruct((R, N), BF16),
        scratch_shapes=[pltpu.VMEM((tm, D_MODEL), BF16)],
        compiler_params=_params("parallel", "arbitrary"),
        name="mem_kv",
    )(mem2, g, w)


def _mem_attn_kernel(q_ref, kv_ref, o_ref):
    scale = X_HEAD_DIM ** -0.5
    for h in range(X_HEADS):
        c = slice(h * X_HEAD_DIM, (h + 1) * X_HEAD_DIM)
        k = kv_ref[:, h * X_HEAD_DIM:(h + 1) * X_HEAD_DIM]
        v = kv_ref[:, X_WIDTH + h * X_HEAD_DIM:X_WIDTH + (h + 1) * X_HEAD_DIM]
        s = lax.dot_general(q_ref[:, c], k, (((1,), (1,)), ((), ())), preferred_element_type=F32) * scale
        m = jnp.max(s, axis=-1, keepdims=True)
        p = jnp.exp(s - m)
        l = jnp.sum(p, axis=-1, keepdims=True)
        o = jnp.dot(p.astype(BF16), v, preferred_element_type=F32)
        o_ref[:, c] = (o / l).astype(BF16)


def _mem_attn(z, kvm, *, seq, mem_len, tm=1024):
    T = z.shape[0]
    return pl.pallas_call(
        _mem_attn_kernel,
        grid=(T // tm,),
        in_specs=[
            pl.BlockSpec((tm, X_WIDTH), lambda i: (i, Z_QX // X_WIDTH)),
            pl.BlockSpec((mem_len, 2 * X_WIDTH), lambda i: (i // (seq // tm), 0)),
        ],
        out_specs=pl.BlockSpec((tm, X_WIDTH), lambda i: (i, 0)),
        out_shape=jax.ShapeDtypeStruct((T, X_WIDTH), BF16),
        compiler_params=_params("parallel"),
        name="mem_attn",
    )(z, kvm)


def _merge_kernel(o_ref, cv_ref, cb_ref, cc_ref, cvp_ref, ccp_ref, cvn_ref, ccn_ref, yx_ref,
                  g0_ref, g1_ref, g2_ref, b_ref, cw_ref,
                  wa_ref, wc_ref, wm_ref, out_ref, pext_ref, *, tm, tn, tiles_per_seq):
    pos_in_seq = pl.program_id(0) % tiles_per_seq
    keep_prev = jnp.where(pos_in_seq == 0, 0.0, 1.0)
    keep_next = jnp.where(pos_in_seq == tiles_per_seq - 1, 0.0, 1.0)
    h8 = SUBLANES_F32
    prev = (cvp_ref[h8:, :].astype(F32) * ccp_ref[h8:, :].astype(F32)) * keep_prev
    nxt = (cvn_ref[:h8, :].astype(F32) * ccn_ref[:h8, :].astype(F32)) * keep_next
    pext_ref[0:h8, :] = prev
    pext_ref[tm + h8:tm + 2 * h8, :] = nxt
    pext_ref[h8:tm + h8, :] = cv_ref[...].astype(F32) * cc_ref[...].astype(F32)
    conv = (cw_ref[0:1, :] * pext_ref[h8 - 1:tm + h8 - 1, :] + cw_ref[1:2, :] * pext_ref[h8:tm + h8, :]
            + cw_ref[2:3, :] * pext_ref[h8 + 1:tm + h8 + 1, :])
    cbr = (cb_ref[...].astype(F32) * conv).astype(BF16)

    def gate(g_ref, k, c):
        return jax.nn.sigmoid(g_ref[:, c].astype(F32) + b_ref[:, k * D_MODEL + c.start:k * D_MODEL + c.stop])

    for n in range(D_MODEL // tn):
        c = slice(n * tn, (n + 1) * tn)
        y = gate(g0_ref, 0, c) * jnp.dot(o_ref[...], wa_ref[:, c], preferred_element_type=F32)
        y += gate(g1_ref, 1, c) * jnp.dot(cbr, wc_ref[:, c], preferred_element_type=F32)
        y += gate(g2_ref, 2, c) * jnp.dot(yx_ref[...], wm_ref[:, c], preferred_element_type=F32)
        out_ref[:, c] = y.astype(BF16)


def _merge(o_mla, z, yx, gate_bias, conv_w, w_o_mla, w_out_conv, w_o_mem, *, seq, tm=512, tn=2048):
    T = z.shape[0]
    hb = SUBLANES_BF16
    rb = tm // hb
    last_hb = T // hb - 1
    cw = CONV_WIDTH
    once = pl.Buffered(1)

    def zcol(c0, width):
        return c0 // width

    def gate_spec(k):
        return pl.BlockSpec((tm, D_MODEL), lambda i, k=k: (i, Z_G // D_MODEL + k))

    def prev_spec(c0):
        return pl.BlockSpec((hb, cw), lambda i: (jnp.maximum(i * rb - 1, 0), zcol(c0, cw)))

    def next_spec(c0):
        return pl.BlockSpec((hb, cw), lambda i: (jnp.minimum((i + 1) * rb, last_hb), zcol(c0, cw)))

    def w_spec():
        return pl.BlockSpec((cw, D_MODEL), lambda i: (0, 0), pipeline_mode=once)

    assert Z_G % D_MODEL == 0
    return pl.pallas_call(
        functools.partial(_merge_kernel, tm=tm, tn=tn, tiles_per_seq=seq // tm),
        grid=(T // tm,),
        in_specs=[
            pl.BlockSpec((tm, MLA_HEADS * V_HEAD), lambda i: (i, 0)),
            pl.BlockSpec((tm, cw), lambda i: (i, zcol(Z_CV, cw))),
            pl.BlockSpec((tm, cw), lambda i: (i, zcol(Z_CB, cw))),
            pl.BlockSpec((tm, cw), lambda i: (i, zcol(Z_CC, cw))),
            prev_spec(Z_CV), prev_spec(Z_CC), next_spec(Z_CV), next_spec(Z_CC),
            pl.BlockSpec((tm, X_WIDTH), lambda i: (i, 0)),
            gate_spec(0), gate_spec(1), gate_spec(2),
            pl.BlockSpec((1, N_BRANCH * D_MODEL), lambda i: (0, 0), pipeline_mode=once),
            pl.BlockSpec((3, cw), lambda i: (0, 0), pipeline_mode=once),
            w_spec(), w_spec(), w_spec(),
        ],
        out_specs=pl.BlockSpec((tm, D_MODEL), lambda i: (i, 0)),
        out_shape=jax.ShapeDtypeStruct((T, D_MODEL), BF16),
        scratch_shapes=[pltpu.VMEM((tm + 2 * SUBLANES_F32, cw), F32)],
        compiler_params=_params("parallel"),
        name="merge",
    )(o_mla, z, z, z, z, z, z, z, yx, z, z, z, gate_bias, conv_w, w_o_mla, w_out_conv, w_o_mem)


def _out_proj_kernel(m_ref, w_ref, x_ref, o_ref):
    o_ref[...] = x_ref[...] + jnp.dot(m_ref[...], w_ref[...], preferred_element_type=F32)


def _out_proj(merged, w_o, x2, *, tm=512):
    T = x2.shape[0]
    return pl.pallas_call(
        _out_proj_kernel,
        grid=(T // tm,),
        in_specs=[
            pl.BlockSpec((tm, D_MODEL), lambda i: (i, 0)),
            pl.BlockSpec((D_MODEL, D_MODEL), lambda i: (0, 0), pipeline_mode=pl.Buffered(1)),
            pl.BlockSpec((tm, D_MODEL), lambda i: (i, 0)),
        ],
        out_specs=pl.BlockSpec((tm, D_MODEL), lambda i: (i, 0)),
        out_shape=jax.ShapeDtypeStruct((T, D_MODEL), F32),
        compiler_params=_params("parallel"),
        name="out_proj",
    )(merged, w_o, x2)


FFN_COL_CHUNK = 256


def _ffn_kernel(x_ref, xp_ref, xn_ref, g_ref, wu_ref, ct_ref, wd_ref, fg_ref,
                o_ref, hn_ref, u_ref, *, tm, tf, tiles_per_seq, n_f, bps, final):
    acc_ref = o_ref
    i = pl.program_id(0)
    s = pl.program_id(1)
    h8 = SUBLANES_F32
    cw = FFN_COL_CHUNK
    rows = 128
    n_chunks = tf // cw
    n_full = n_f // bps
    n_steps = pl.cdiv(n_f, bps)

    @pl.when(s == 0)
    def _():
        pos_in_seq = i % tiles_per_seq
        keep_prev = jnp.where(pos_in_seq == 0, 0.0, 1.0)
        keep_next = jnp.where(pos_in_seq == tiles_per_seq - 1, 0.0, 1.0)
        halo = jnp.concatenate([_rms(xp_ref[...], g_ref[...]) * keep_prev,
                                _rms(xn_ref[...], g_ref[...]) * keep_next], axis=0)
        hn_ref[tm:tm + 2 * h8, :] = halo.astype(BF16)
        _rms_rows_to(x_ref, g_ref, hn_ref, tm, rows)
        acc_ref[...] = jnp.zeros_like(acc_ref)

    def up(fb, u_ref):
        r = jnp.dot(hn_ref[...], wu_ref[fb], preferred_element_type=F32)
        u_ref[h8:tm + h8, :] = r[:tm]
        u_ref[0:h8, :] = r[tm:tm + h8]
        u_ref[tm + h8:tm + 2 * h8, :] = r[tm + h8:]

    def conv(u_ref, fb, c0):
        return (ct_ref[fb, 0:1, c0:c0 + cw] * u_ref[h8 - 1:tm + h8 - 1, c0:c0 + cw]
                + ct_ref[fb, 1:2, c0:c0 + cw] * u_ref[h8:tm + h8, c0:c0 + cw]
                + ct_ref[fb, 2:3, c0:c0 + cw] * u_ref[h8 + 1:tm + h8 + 1, c0:c0 + cw])

    def block(fb):
        up(fb, u_ref.at[fb])
        down = None
        for n in range(n_chunks):
            c0 = n * cw
            a = conv(u_ref.at[fb], fb, c0)
            b = conv(u_ref.at[fb], fb, tf + c0)
            act = (a * jax.nn.sigmoid(a) * b).astype(BF16)
            d = jnp.dot(act, wd_ref[fb * tf + c0:fb * tf + c0 + cw, :], preferred_element_type=F32)
            down = d if down is None else down + d
        acc_ref[...] += down

    def blocks(count):
        for fb in range(count):
            block(fb)

    pl.when(s < n_full)(functools.partial(blocks, bps))
    if n_f % bps:
        pl.when(s == n_full)(functools.partial(blocks, n_f % bps))

    @pl.when(s == n_steps - 1)
    def _():
        def body(c, carry):
            r = pl.ds(pl.multiple_of(c * rows, rows), rows)
            y = x_ref[r, :] + acc_ref[r, :]
            o_ref[r, :] = _rms(y, fg_ref[...]) if final else y
            return carry
        lax.fori_loop(0, tm // rows, body, 0)


def _ffn(x1, ffn_norm, w_up_blk, conv_taps, w_down, final_norm, *, seq, final, tm=512, bps=2):
    T = x1.shape[0]
    h8 = SUBLANES_F32
    rb = tm // h8
    last_hb = T // h8 - 1
    n_f = w_up_blk.shape[0]
    tf = w_up_blk.shape[2] // 2
    return pl.pallas_call(
        functools.partial(_ffn_kernel, tm=tm, tf=tf, tiles_per_seq=seq // tm, n_f=n_f, bps=bps, final=final),
        grid=(T // tm, pl.cdiv(n_f, bps)),
        in_specs=[
            pl.BlockSpec((tm, D_MODEL), lambda i, s: (i, 0)),
            pl.BlockSpec((h8, D_MODEL), lambda i, s: (jnp.maximum(i * rb - 1, 0), 0)),
            pl.BlockSpec((h8, D_MODEL), lambda i, s: (jnp.minimum((i + 1) * rb, last_hb), 0)),
            pl.BlockSpec((1, D_MODEL), lambda i, s: (0, 0)),
            pl.BlockSpec((bps, D_MODEL, 2 * tf), lambda i, s: (s, 0, 0)),
            pl.BlockSpec((bps, 3, 2 * tf), lambda i, s: (s, 0, 0)),
            pl.BlockSpec((bps * tf, D_MODEL), lambda i, s: (s, 0)),
            pl.BlockSpec((1, D_MODEL), lambda i, s: (0, 0)),
        ],
        out_specs=pl.BlockSpec((tm, D_MODEL), lambda i, s: (i, 0)),
        out_shape=jax.ShapeDtypeStruct((T, D_MODEL), F32),
        scratch_shapes=[
            pltpu.VMEM((tm + 2 * h8, D_MODEL), BF16),
            pltpu.VMEM((bps, tm + 2 * h8, 2 * tf), F32),
        ],
        compiler_params=_params("parallel", "arbitrary"),
        name="ffn",
    )(x1, x1, x1, ffn_norm, w_up_blk, conv_taps, w_down, final_norm)


def _block_conv_taps(ffn_conv_w):
    k = ffn_conv_w.shape[0]
    n_f = D_FF // FFN_TF
    halves = ffn_conv_w.reshape(k, 2, n_f, FFN_TF)
    return jnp.transpose(halves, (2, 0, 1, 3)).reshape(n_f, k, 2 * FFN_TF)


def _pack_w_uq(w_uq):
    w = w_uq.reshape(Q_LORA, MLA_HEADS, QK_NOPE + QK_ROPE)
    nope = (w[:, :, :QK_NOPE] * Q_SCALE).reshape(Q_LORA, MLA_HEADS * QK_NOPE)
    rope = jnp.pad(w[:, :, QK_NOPE:], ((0, 0), (0, 0), (0, LANES - QK_ROPE))).reshape(Q_LORA, MLA_HEADS * LANES)
    return jnp.concatenate([nope, rope], axis=1).astype(BF16)


def kernel(x, mem, positions, mix_norm, w_in, q_norm, w_uq, kv_norm, w_ukv, w_o_mla, conv_w, w_out_conv,
           mem_norm, w_mem_kv, w_o_mem, gate_bias, w_o, ffn_norm, w_up, ffn_conv_w, w_down, final_norm):
    B, S, D = x.shape
    M = mem.shape[1]
    T = B * S
    depth = w_in.shape[0]
    x2 = x.reshape(T, D)
    mem2 = mem.reshape(B * M, D)
    pos = positions.reshape(T, 1)
    inv_freq = jnp.power(ROPE_THETA, -jnp.arange(0, QK_ROPE, 2, dtype=F32) / QK_ROPE)
    invf = jnp.concatenate([inv_freq, inv_freq, jnp.zeros((LANES - QK_ROPE,), F32)]).reshape(1, LANES)

    for l in range(depth):
        w_blk, w_kr = _pack_w_in(jnp.swapaxes(w_in[l], 0, 1))
        z, kr, w_up_blk, b_down = _in_proj(x2, mix_norm[l].reshape(1, D), w_blk, w_kr, w_up[l], w_down[l])
        q, k, v, (b_mem_kv, b_o_mla, b_out_conv, b_o_mem, b_o) = _qkv(
            z, kr, pos, invf, q_norm[l].reshape(1, -1), kv_norm[l].reshape(1, -1),
            _pack_w_uq(w_uq[l]), w_ukv[l].astype(BF16),
            (w_mem_kv[l], w_o_mla[l], w_out_conv[l], w_o_mem[l], w_o[l]))
        o_mla = _mla(q, k, v, batch=B, seq=S)
        kvm = _mem_kv(mem2, mem_norm[l].reshape(1, D), b_mem_kv)
        yx = _mem_attn(z, kvm, seq=S, mem_len=M)
        merged = _merge(o_mla, z, yx, gate_bias[l].reshape(1, -1), conv_w[l], b_o_mla, b_out_conv, b_o_mem, seq=S)
        x2 = _out_proj(merged, b_o, x2)
        x2 = _ffn(x2, ffn_norm[l].reshape(1, D), w_up_blk, _block_conv_taps(ffn_conv_w[l]),
                  b_down, final_norm.reshape(1, D), seq=S, final=(l == depth - 1))
    return x2.reshape(B, S, D)
```

```python
import functools

import jax
import jax.numpy as jnp
from jax import lax
from jax.experimental import pallas as pl
from jax.experimental.pallas import tpu as pltpu

F32 = jnp.float32
BF16 = jnp.bfloat16

D_MODEL = 2048
MLA_HEADS = 8
Q_LORA = 512
KV_LORA = 512
QK_NOPE = 128
QK_ROPE = 64
V_HEAD = 128
ROPE_THETA = 10000.0
CONV_WIDTH = 1024
X_HEADS = 4
X_HEAD_DIM = 256
X_WIDTH = X_HEADS * X_HEAD_DIM
D_FF = 5632
N_BRANCH = 3
EPS = 1e-6
LOG2_E = 1.4426950408889634
Q_SCALE = (QK_NOPE + QK_ROPE) ** -0.5 * LOG2_E

LANES = 128
SUBLANES_F32 = 8
SUBLANES_BF16 = 16
HEAD_PAD = 256
VMEM_LIMIT = 56 * 1024 * 1024

Z_G = 0
Z_CQ = Z_G + N_BRANCH * D_MODEL
Z_CKV = Z_CQ + Q_LORA
Z_CV = Z_CKV + KV_LORA
Z_CB = Z_CV + CONV_WIDTH
Z_CC = Z_CB + CONV_WIDTH
Z_QX = Z_CC + CONV_WIDTH
Z_COLS = Z_QX + X_WIDTH


_NT = (((1,), (1,)), ((), ()))


def _params(*sem):
    return pltpu.CompilerParams(dimension_semantics=sem, vmem_limit_bytes=VMEM_LIMIT)


def _rms(x, g):
    inv = lax.rsqrt(jnp.mean(x * x, axis=-1, keepdims=True) + EPS)
    return x * inv * g


def _rms_rows_to(x_ref, g_ref, out_ref, rows, chunk, out_row0=0):
    def body(c, carry):
        r0 = pl.multiple_of(c * chunk, chunk)
        x = x_ref[pl.ds(r0, chunk), :].astype(F32)
        out_ref[pl.ds(out_row0 + r0, chunk), :] = _rms(x, g_ref[...]).astype(out_ref.dtype)
        return carry
    lax.fori_loop(0, rows // chunk, body, 0)


IN_TN = 1024
FFN_TF = 512


N_GATE_BLK = N_BRANCH * D_MODEL // IN_TN


def _pack_w_in_kernel(a_ref, b_ref, kr_ref, wblk_ref, wkr_ref):
    j = pl.program_id(0)

    @pl.when(j == N_GATE_BLK)
    def _():
        wblk_ref[0] = a_ref[...].astype(BF16)

    @pl.when(j != N_GATE_BLK)
    def _():
        wblk_ref[0, :IN_TN - QK_ROPE, :] = a_ref[QK_ROPE:, :].astype(BF16)
        wblk_ref[0, IN_TN - QK_ROPE:, :] = b_ref[...].astype(BF16)

    @pl.when(j == 0)
    def _():
        wkr_ref[:QK_ROPE, :] = kr_ref[...].astype(BF16)
        wkr_ref[QK_ROPE:, :] = jnp.zeros((LANES - QK_ROPE, wkr_ref.shape[1]), BF16)


def _pack_w_in(wi_t):
    cols, d = wi_t.shape
    kr0 = Q_LORA + KV_LORA
    assert cols == Z_COLS + QK_ROPE and kr0 == IN_TN and IN_TN % QK_ROPE == 0
    n_blk = Z_COLS // IN_TN
    n_tail = n_blk - 1 - N_GATE_BLK
    sub = IN_TN // QK_ROPE

    def src_blk(j):
        return jnp.where(j < N_GATE_BLK, j + 1 + n_tail, jnp.where(j == N_GATE_BLK, 0, j - N_GATE_BLK))

    return pl.pallas_call(
        _pack_w_in_kernel,
        grid=(n_blk,),
        in_specs=[
            pl.BlockSpec((IN_TN, d), lambda j: (src_blk(j), 0)),
            pl.BlockSpec((QK_ROPE, d), lambda j: ((src_blk(j) + 1) * sub, 0)),
            pl.BlockSpec((QK_ROPE, d), lambda j: (kr0 // QK_ROPE, 0)),
        ],
        out_specs=[
            pl.BlockSpec((1, IN_TN, d), lambda j: (j, 0, 0)),
            pl.BlockSpec((LANES, d), lambda j: (0, 0)),
        ],
        out_shape=[
            jax.ShapeDtypeStruct((n_blk, IN_TN, d), BF16),
            jax.ShapeDtypeStruct((LANES, d), BF16),
        ],
        compiler_params=_params("arbitrary"),
        name="pack_w_in",
    )(wi_t, wi_t, wi_t)


def _in_proj_kernel(x_ref, g_ref, w_ref, wkr_ref, wu_ref, wd_ref, z_ref, kr_ref, wub_ref, wdb_ref, h_ref, *, tm):
    @pl.when(pl.program_id(1) == 0)
    def _():
        _rms_rows_to(x_ref, g_ref, h_ref, tm, 128)
        kr_ref[...] = lax.dot_general(h_ref[...], wkr_ref[...], _NT, preferred_element_type=F32)

    z_ref[...] = lax.dot_general(h_ref[...], w_ref[0], _NT, preferred_element_type=F32).astype(BF16)
    wub_ref[0] = wu_ref[...].astype(BF16)
    wdb_ref[...] = wd_ref[...].astype(BF16)


def _in_proj(x2, g, w_blk, w_kr, w_up, w_down, *, tm=1024):
    T = x2.shape[0]
    tn = IN_TN
    n_i, n_j = T // tm, Z_COLS // tn
    n_steps = n_i * n_j
    d = w_up.shape[0]
    n_f = D_FF // FFN_TF
    up_rows = d * 2 * n_f // n_steps
    down_rows = w_down.shape[0] // n_steps
    assert up_rows * n_steps == d * 2 * n_f and d % up_rows == 0 and up_rows % SUBLANES_BF16 == 0
    assert down_rows * n_steps == w_down.shape[0] and down_rows % SUBLANES_BF16 == 0
    r_per_col = d // up_rows

    def up_in(i, j):
        step = i * n_j + j
        return (step % r_per_col, step // r_per_col)

    def up_out(i, j):
        step = i * n_j + j
        col = step // r_per_col
        return (col % n_f, step % r_per_col, col // n_f)

    return pl.pallas_call(
        functools.partial(_in_proj_kernel, tm=tm),
        grid=(n_i, n_j),
        in_specs=[
            pl.BlockSpec((tm, D_MODEL), lambda i, j: (i, 0)),
            pl.BlockSpec((1, D_MODEL), lambda i, j: (0, 0)),
            pl.BlockSpec((1, tn, D_MODEL), lambda i, j: (j, 0, 0)),
            pl.BlockSpec((LANES, D_MODEL), lambda i, j: (0, 0)),
            pl.BlockSpec((up_rows, FFN_TF), up_in),
            pl.BlockSpec((down_rows, D_MODEL), lambda i, j: (i * n_j + j, 0)),
        ],
        out_specs=[
            pl.BlockSpec((tm, tn), lambda i, j: (i, j)),
            pl.BlockSpec((tm, LANES), lambda i, j: (i, 0)),
            pl.BlockSpec((1, up_rows, FFN_TF), up_out),
            pl.BlockSpec((down_rows, D_MODEL), lambda i, j: (i * n_j + j, 0)),
        ],
        out_shape=[
            jax.ShapeDtypeStruct((T, Z_COLS), BF16),
            jax.ShapeDtypeStruct((T, LANES), F32),
            jax.ShapeDtypeStruct((n_f, d, 2 * FFN_TF), BF16),
            jax.ShapeDtypeStruct(w_down.shape, BF16),
        ],
        scratch_shapes=[pltpu.VMEM((tm, D_MODEL), BF16)],
        compiler_params=_params("arbitrary", "arbitrary"),
        name="in_proj",
    )(x2, g, w_blk, w_kr, w_up, w_down)


def _qkv_kernel(cq_ref, ckv_ref, kr_ref, pos_ref, invf_ref, qn_ref, kvn_ref, wuq_ref, wukv_ref, *rest, tm):
    n_w = (len(rest) - 3) // 2
    q_ref, k_ref, v_ref = rest[n_w:n_w + 3]
    for w_ref, b_ref in zip(rest[:n_w], rest[n_w + 3:]):
        b_ref[...] = w_ref[...].astype(BF16)

    ang = pos_ref[...].astype(F32) * invf_ref[...]
    cos = jnp.cos(ang)
    sin = jnp.sin(ang)
    lane = lax.broadcasted_iota(jnp.int32, (tm, LANES), 1)
    half = QK_ROPE // 2
    c_tab = jnp.where(lane < QK_ROPE, cos, 0.0)
    s_lo = jnp.where(lane < half, -sin, 0.0)
    s_hi = jnp.where((lane >= half) & (lane < QK_ROPE), sin, 0.0)

    def rope(t, tabs):
        c, lo, hi = tabs
        return t * c + pltpu.roll(t, LANES - half, 1) * lo + pltpu.roll(t, half, 1) * hi

    k_tabs = (c_tab, s_lo, s_hi)
    q_tabs = tuple(t * Q_SCALE for t in k_tabs)
    cqn = _rms(cq_ref[...].astype(F32), qn_ref[...]).astype(BF16)
    q = jnp.dot(cqn, wuq_ref[...], preferred_element_type=F32)
    nope_cols = MLA_HEADS * QK_NOPE
    for h in range(MLA_HEADS):
        q_ref[:, h * HEAD_PAD:h * HEAD_PAD + QK_NOPE] = q[:, h * QK_NOPE:(h + 1) * QK_NOPE].astype(BF16)
        q_ref[:, h * HEAD_PAD + QK_NOPE:(h + 1) * HEAD_PAD] = (
            rope(q[:, nope_cols + h * LANES:nope_cols + (h + 1) * LANES], q_tabs)).astype(BF16)

    ckvn = _rms(ckv_ref[...].astype(F32), kvn_ref[...]).astype(BF16)
    kv = jnp.dot(ckvn, wukv_ref[...], preferred_element_type=F32)
    k_rope = rope(kr_ref[...], k_tabs).astype(BF16)
    ones = jnp.ones((tm, V_HEAD), BF16)
    for h in range(MLA_HEADS):
        k_ref[:, h * HEAD_PAD:h * HEAD_PAD + QK_NOPE] = kv[:, h * HEAD_PAD:h * HEAD_PAD + QK_NOPE].astype(BF16)
        k_ref[:, h * HEAD_PAD + QK_NOPE:(h + 1) * HEAD_PAD] = k_rope
        v_ref[:, 2 * h * V_HEAD:(2 * h + 1) * V_HEAD] = kv[:, h * HEAD_PAD + QK_NOPE:(h + 1) * HEAD_PAD].astype(BF16)
        v_ref[:, (2 * h + 1) * V_HEAD:(2 * h + 2) * V_HEAD] = ones


def _qkv(z, kr, pos, invf, q_norm, kv_norm, w_uq_p, w_ukv, later_ws, *, tm=512):
    T = z.shape[0]
    qk_cols = MLA_HEADS * HEAD_PAD
    n_steps = T // tm
    assert all(w.shape[0] % (n_steps * SUBLANES_BF16) == 0 for w in later_ws)

    def slice_spec(w):
        return pl.BlockSpec((w.shape[0] // n_steps, w.shape[1]), lambda i: (i, 0))

    outs = pl.pallas_call(
        functools.partial(_qkv_kernel, tm=tm),
        grid=(n_steps,),
        in_specs=[
            pl.BlockSpec((tm, Q_LORA), lambda i: (i, Z_CQ // Q_LORA)),
            pl.BlockSpec((tm, KV_LORA), lambda i: (i, Z_CKV // KV_LORA)),
            pl.BlockSpec((tm, LANES), lambda i: (i, 0)),
            pl.BlockSpec((tm, 1), lambda i: (i, 0)),
            pl.BlockSpec((1, LANES), lambda i: (0, 0)),
            pl.BlockSpec((1, Q_LORA), lambda i: (0, 0)),
            pl.BlockSpec((1, KV_LORA), lambda i: (0, 0)),
            pl.BlockSpec((Q_LORA, qk_cols), lambda i: (0, 0)),
            pl.BlockSpec((KV_LORA, qk_cols), lambda i: (0, 0)),
        ] + [slice_spec(w) for w in later_ws],
        out_specs=[
            pl.BlockSpec((tm, qk_cols), lambda i: (i, 0)),
            pl.BlockSpec((tm, qk_cols), lambda i: (i, 0)),
            pl.BlockSpec((tm, 2 * MLA_HEADS * V_HEAD), lambda i: (i, 0)),
        ] + [slice_spec(w) for w in later_ws],
        out_shape=[
            jax.ShapeDtypeStruct((T, qk_cols), BF16),
            jax.ShapeDtypeStruct((T, qk_cols), BF16),
            jax.ShapeDtypeStruct((T, 2 * MLA_HEADS * V_HEAD), BF16),
        ] + [jax.ShapeDtypeStruct(w.shape, BF16) for w in later_ws],
        compiler_params=_params("arbitrary"),
        name="qkv",
    )(z, z, kr, pos, invf, q_norm, kv_norm, w_uq_p, w_ukv, *later_ws)
    return outs[0], outs[1], outs[2], outs[3:]


def _mla_kernel(q_ref, k_ref, v_ref, o_ref, *, seq, tq, heads):
    for h in range(heads):
        k = k_ref[:, h * HEAD_PAD:(h + 1) * HEAD_PAD]
        v = v_ref[:, 2 * h * V_HEAD:2 * (h + 1) * V_HEAD]
        for c in range(seq // tq):
            r = slice(c * tq, (c + 1) * tq)
            s = lax.dot_general(q_ref[r, h * HEAD_PAD:(h + 1) * HEAD_PAD], k, (((1,), (1,)), ((), ())),
                                preferred_element_type=F32)
            m = jnp.max(s, axis=-1, keepdims=True)
            p = jnp.exp2(s - m).astype(BF16)
            ov = jnp.dot(p, v, preferred_element_type=F32)
            o_ref[r, h * V_HEAD:(h + 1) * V_HEAD] = (ov[:, :V_HEAD] / ov[:, V_HEAD:]).astype(BF16)


def _mla(q, k, v, *, batch, seq, tq=512, heads=4):
    T = q.shape[0]
    return pl.pallas_call(
        functools.partial(_mla_kernel, seq=seq, tq=tq, heads=heads),
        grid=(batch, MLA_HEADS // heads),
        in_specs=[
            pl.BlockSpec((seq, heads * HEAD_PAD), lambda b, h: (b, h)),
            pl.BlockSpec((seq, heads * HEAD_PAD), lambda b, h: (b, h)),
            pl.BlockSpec((seq, heads * 2 * V_HEAD), lambda b, h: (b, h)),
        ],
        out_specs=pl.BlockSpec((seq, heads * V_HEAD), lambda b, h: (b, h)),
        out_shape=jax.ShapeDtypeStruct((T, MLA_HEADS * V_HEAD), BF16),
        compiler_params=_params("parallel", "parallel"),
        name="mla_attn",
    )(q, k, v)


def _mem_kv_kernel(m_ref, g_ref, w_ref, o_ref, h_ref, *, tm):
    @pl.when(pl.program_id(1) == 0)
    def _():
        _rms_rows_to(m_ref, g_ref, h_ref, tm, 128)

    o_ref[...] = jnp.dot(h_ref[...], w_ref[...], preferred_element_type=F32).astype(BF16)


def _mem_kv(mem2, g, w, *, tm=1024, tn=1024):
    R = mem2.shape[0]
    N = w.shape[1]
    return pl.pallas_call(
        functools.partial(_mem_kv_kernel, tm=tm),
        grid=(R // tm, N // tn),
        in_specs=[
            pl.BlockSpec((tm, D_MODEL), lambda i, j: (i, 0)),
            pl.BlockSpec((1, D_MODEL), lambda i, j: (0, 0)),
            pl.BlockSpec((D_MODEL, tn), lambda i, j: (0, j)),
        ],
        out_specs=pl.BlockSpec((tm, tn), lambda i, j: (i, j)),
        out_shape=jax.ShapeDtypeStruct((R, N), BF16),
        scratch_shapes=[pltpu.VMEM((tm, D_MODEL), BF16)],
        compiler_params=_params("parallel", "arbitrary"),
        name="mem_kv",
    )(mem2, g, w)


def _mem_attn_kernel(q_ref, kv_ref, o_ref):
    scale = X_HEAD_DIM ** -0.5
    for h in range(X_HEADS):
        c = slice(h * X_HEAD_DIM, (h + 1) * X_HEAD_DIM)
        k = kv_ref[:, h * X_HEAD_DIM:(h + 1) * X_HEAD_DIM]
        v = kv_ref[:, X_WIDTH + h * X_HEAD_DIM:X_WIDTH + (h + 1) * X_HEAD_DIM]
        s = lax.dot_general(q_ref[:, c], k, (((1,), (1,)), ((), ())), preferred_element_type=F32) * scale
        m = jnp.max(s, axis=-1, keepdims=True)
        p = jnp.exp(s - m)
        l = jnp.sum(p, axis=-1, keepdims=True)
        o = jnp.dot(p.astype(BF16), v, preferred_element_type=F32)
        o_ref[:, c] = (o / l).astype(BF16)


def _mem_attn(z, kvm, *, seq, mem_len, tm=1024):
    T = z.shape[0]
    return pl.pallas_call(
        _mem_attn_kernel,
        grid=(T // tm,),
        in_specs=[
            pl.BlockSpec((tm, X_WIDTH), lambda i: (i, Z_QX // X_WIDTH)),
            pl.BlockSpec((mem_len, 2 * X_WIDTH), lambda i: (i // (seq // tm), 0)),
        ],
        out_specs=pl.BlockSpec((tm, X_WIDTH), lambda i: (i, 0)),
        out_shape=jax.ShapeDtypeStruct((T, X_WIDTH), BF16),
        compiler_params=_params("parallel"),
        name="mem_attn",
    )(z, kvm)


def _merge_kernel(o_ref, cv_ref, cb_ref, cc_ref, cvp_ref, ccp_ref, cvn_ref, ccn_ref, qx_ref, kvm_ref,
                  g0_ref, g1_ref, g2_ref, b_ref, cw_ref,
                  wa_ref, wc_ref, wm_ref, out_ref, pext_ref, *, tm, tn, tiles_per_seq):
    heads = []
    for h in range(X_HEADS):
        hc = slice(h * X_HEAD_DIM, (h + 1) * X_HEAD_DIM)
        k = kvm_ref[:, h * X_HEAD_DIM:(h + 1) * X_HEAD_DIM]
        v = kvm_ref[:, X_WIDTH + h * X_HEAD_DIM:X_WIDTH + (h + 1) * X_HEAD_DIM]
        s = lax.dot_general(qx_ref[:, hc], k, _NT, preferred_element_type=F32) * (X_HEAD_DIM ** -0.5)
        p = jnp.exp(s - jnp.max(s, axis=-1, keepdims=True))
        l = jnp.sum(p, axis=-1, keepdims=True)
        heads.append((jnp.dot(p.astype(BF16), v, preferred_element_type=F32) / l).astype(BF16))
    yx = jnp.concatenate(heads, axis=1)

    pos_in_seq = pl.program_id(0) % tiles_per_seq
    keep_prev = jnp.where(pos_in_seq == 0, 0.0, 1.0)
    keep_next = jnp.where(pos_in_seq == tiles_per_seq - 1, 0.0, 1.0)
    h8 = SUBLANES_F32
    prev = (cvp_ref[h8:, :].astype(F32) * ccp_ref[h8:, :].astype(F32)) * keep_prev
    nxt = (cvn_ref[:h8, :].astype(F32) * ccn_ref[:h8, :].astype(F32)) * keep_next
    pext_ref[0:h8, :] = prev
    pext_ref[tm + h8:tm + 2 * h8, :] = nxt
    pext_ref[h8:tm + h8, :] = cv_ref[...].astype(F32) * cc_ref[...].astype(F32)
    conv = (cw_ref[0:1, :] * pext_ref[h8 - 1:tm + h8 - 1, :] + cw_ref[1:2, :] * pext_ref[h8:tm + h8, :]
            + cw_ref[2:3, :] * pext_ref[h8 + 1:tm + h8 + 1, :])
    cbr = (cb_ref[...].astype(F32) * conv).astype(BF16)

    def gate(g_ref, k, c):
        return jax.nn.sigmoid(g_ref[:, c].astype(F32) + b_ref[:, k * D_MODEL + c.start:k * D_MODEL + c.stop])

    for n in range(D_MODEL // tn):
        c = slice(n * tn, (n + 1) * tn)
        y = gate(g0_ref, 0, c) * jnp.dot(o_ref[...], wa_ref[:, c], preferred_element_type=F32)
        y += gate(g1_ref, 1, c) * jnp.dot(cbr, wc_ref[:, c], preferred_element_type=F32)
        y += gate(g2_ref, 2, c) * jnp.dot(yx, wm_ref[:, c], preferred_element_type=F32)
        out_ref[:, c] = y.astype(BF16)


def _merge(o_mla, z, kvm, gate_bias, conv_w, w_o_mla, w_out_conv, w_o_mem, *, seq, mem_len, tm=512, tn=2048):
    T = z.shape[0]
    hb = SUBLANES_BF16
    rb = tm // hb
    last_hb = T // hb - 1
    cw = CONV_WIDTH
    once = pl.Buffered(1)

    def zcol(c0, width):
        return c0 // width

    def gate_spec(k):
        return pl.BlockSpec((tm, D_MODEL), lambda i, k=k: (i, Z_G // D_MODEL + k))

    def prev_spec(c0):
        return pl.BlockSpec((hb, cw), lambda i: (jnp.maximum(i * rb - 1, 0), zcol(c0, cw)))

    def next_spec(c0):
        return pl.BlockSpec((hb, cw), lambda i: (jnp.minimum((i + 1) * rb, last_hb), zcol(c0, cw)))

    def w_spec():
        return pl.BlockSpec((cw, D_MODEL), lambda i: (0, 0), pipeline_mode=once)

    assert Z_G % D_MODEL == 0
    return pl.pallas_call(
        functools.partial(_merge_kernel, tm=tm, tn=tn, tiles_per_seq=seq // tm),
        grid=(T // tm,),
        in_specs=[
            pl.BlockSpec((tm, MLA_HEADS * V_HEAD), lambda i: (i, 0)),
            pl.BlockSpec((tm, cw), lambda i: (i, zcol(Z_CV, cw))),
            pl.BlockSpec((tm, cw), lambda i: (i, zcol(Z_CB, cw))),
            pl.BlockSpec((tm, cw), lambda i: (i, zcol(Z_CC, cw))),
            prev_spec(Z_CV), prev_spec(Z_CC), next_spec(Z_CV), next_spec(Z_CC),
            pl.BlockSpec((tm, X_WIDTH), lambda i: (i, Z_QX // X_WIDTH)),
            pl.BlockSpec((mem_len, 2 * X_WIDTH), lambda i: (i // (seq // tm), 0)),
            gate_spec(0), gate_spec(1), gate_spec(2),
            pl.BlockSpec((1, N_BRANCH * D_MODEL), lambda i: (0, 0), pipeline_mode=once),
            pl.BlockSpec((3, cw), lambda i: (0, 0), pipeline_mode=once),
            w_spec(), w_spec(), w_spec(),
        ],
        out_specs=pl.BlockSpec((tm, D_MODEL), lambda i: (i, 0)),
        out_shape=jax.ShapeDtypeStruct((T, D_MODEL), BF16),
        scratch_shapes=[pltpu.VMEM((tm + 2 * SUBLANES_F32, cw), F32)],
        compiler_params=_params("parallel"),
        name="merge",
    )(o_mla, z, z, z, z, z, z, z, z, kvm, z, z, z, gate_bias, conv_w, w_o_mla, w_out_conv, w_o_mem)


def _out_proj_kernel(m_ref, w_ref, x_ref, o_ref):
    o_ref[...] = x_ref[...] + jnp.dot(m_ref[...], w_ref[...], preferred_element_type=F32)


def _out_proj(merged, w_o, x2, *, tm=512):
    T = x2.shape[0]
    return pl.pallas_call(
        _out_proj_kernel,
        grid=(T // tm,),
        in_specs=[
            pl.BlockSpec((tm, D_MODEL), lambda i: (i, 0)),
            pl.BlockSpec((D_MODEL, D_MODEL), lambda i: (0, 0), pipeline_mode=pl.Buffered(1)),
            pl.BlockSpec((tm, D_MODEL), lambda i: (i, 0)),
        ],
        out_specs=pl.BlockSpec((tm, D_MODEL), lambda i: (i, 0)),
        out_shape=jax.ShapeDtypeStruct((T, D_MODEL), F32),
        compiler_params=_params("parallel"),
        name="out_proj",
    )(merged, w_o, x2)


FFN_COL_CHUNK = 256


def _ffn_kernel(x_ref, xp_ref, xn_ref, g_ref, wu_ref, ct_ref, wd_ref, fg_ref,
                o_ref, hn_ref, u_ref, *, tm, tf, tiles_per_seq, n_f, bps, final):
    acc_ref = o_ref
    i = pl.program_id(0)
    s = pl.program_id(1)
    h8 = SUBLANES_F32
    cw = FFN_COL_CHUNK
    rows = 128
    n_chunks = tf // cw
    n_full = n_f // bps
    n_steps = pl.cdiv(n_f, bps)

    @pl.when(s == 0)
    def _():
        pos_in_seq = i % tiles_per_seq
        keep_prev = jnp.where(pos_in_seq == 0, 0.0, 1.0)
        keep_next = jnp.where(pos_in_seq == tiles_per_seq - 1, 0.0, 1.0)
        halo = jnp.concatenate([_rms(xp_ref[...], g_ref[...]) * keep_prev,
                                _rms(xn_ref[...], g_ref[...]) * keep_next], axis=0)
        hn_ref[tm:tm + 2 * h8, :] = halo.astype(BF16)
        _rms_rows_to(x_ref, g_ref, hn_ref, tm, rows)
        acc_ref[...] = jnp.zeros_like(acc_ref)

    def up(fb, u_ref):
        r = jnp.dot(hn_ref[...], wu_ref[fb], preferred_element_type=F32)
        u_ref[h8:tm + h8, :] = r[:tm]
        u_ref[0:h8, :] = r[tm:tm + h8]
        u_ref[tm + h8:tm + 2 * h8, :] = r[tm + h8:]

    def conv(u_ref, fb, c0):
        return (ct_ref[fb, 0:1, c0:c0 + cw] * u_ref[h8 - 1:tm + h8 - 1, c0:c0 + cw]
                + ct_ref[fb, 1:2, c0:c0 + cw] * u_ref[h8:tm + h8, c0:c0 + cw]
                + ct_ref[fb, 2:3, c0:c0 + cw] * u_ref[h8 + 1:tm + h8 + 1, c0:c0 + cw])

    def block(fb):
        up(fb, u_ref.at[fb])
        down = None
        for n in range(n_chunks):
            c0 = n * cw
            a = conv(u_ref.at[fb], fb, c0)
            b = conv(u_ref.at[fb], fb, tf + c0)
            act = (a * jax.nn.sigmoid(a) * b).astype(BF16)
            d = jnp.dot(act, wd_ref[fb * tf + c0:fb * tf + c0 + cw, :], preferred_element_type=F32)
            down = d if down is None else down + d
        acc_ref[...] += down

    def blocks(count):
        for fb in range(count):
            block(fb)

    pl.when(s < n_full)(functools.partial(blocks, bps))
    if n_f % bps:
        pl.when(s == n_full)(functools.partial(blocks, n_f % bps))

    @pl.when(s == n_steps - 1)
    def _():
        def body(c, carry):
            r = pl.ds(pl.multiple_of(c * rows, rows), rows)
            y = x_ref[r, :] + acc_ref[r, :]
            o_ref[r, :] = _rms(y, fg_ref[...]) if final else y
            return carry
        lax.fori_loop(0, tm // rows, body, 0)


def _ffn(x1, ffn_norm, w_up_blk, conv_taps, w_down, final_norm, *, seq, final, tm=512, bps=2):
    T = x1.shape[0]
    h8 = SUBLANES_F32
    rb = tm // h8
    last_hb = T // h8 - 1
    n_f = w_up_blk.shape[0]
    tf = w_up_blk.shape[2] // 2
    return pl.pallas_call(
        functools.partial(_ffn_kernel, tm=tm, tf=tf, tiles_per_seq=seq // tm, n_f=n_f, bps=bps, final=final),
        grid=(T // tm, pl.cdiv(n_f, bps)),
        in_specs=[
            pl.BlockSpec((tm, D_MODEL), lambda i, s: (i, 0)),
            pl.BlockSpec((h8, D_MODEL), lambda i, s: (jnp.maximum(i * rb - 1, 0), 0)),
            pl.BlockSpec((h8, D_MODEL), lambda i, s: (jnp.minimum((i + 1) * rb, last_hb), 0)),
            pl.BlockSpec((1, D_MODEL), lambda i, s: (0, 0)),
            pl.BlockSpec((bps, D_MODEL, 2 * tf), lambda i, s: (s, 0, 0)),
            pl.BlockSpec((bps, 3, 2 * tf), lambda i, s: (s, 0, 0)),
            pl.BlockSpec((bps * tf, D_MODEL), lambda i, s: (s, 0)),
            pl.BlockSpec((1, D_MODEL), lambda i, s: (0, 0)),
        ],
        out_specs=pl.BlockSpec((tm, D_MODEL), lambda i, s: (i, 0)),
        out_shape=jax.ShapeDtypeStruct((T, D_MODEL), F32),
        scratch_shapes=[
            pltpu.VMEM((tm + 2 * h8, D_MODEL), BF16),
            pltpu.VMEM((bps, tm + 2 * h8, 2 * tf), F32),
        ],
        compiler_params=_params("parallel", "arbitrary"),
        name="ffn",
    )(x1, x1, x1, ffn_norm, w_up_blk, conv_taps, w_down, final_norm)


def _block_conv_taps(ffn_conv_w):
    k = ffn_conv_w.shape[0]
    n_f = D_FF // FFN_TF
    halves = ffn_conv_w.reshape(k, 2, n_f, FFN_TF)
    return jnp.transpose(halves, (2, 0, 1, 3)).reshape(n_f, k, 2 * FFN_TF)


def _pack_w_uq(w_uq):
    w = w_uq.reshape(Q_LORA, MLA_HEADS, QK_NOPE + QK_ROPE)
    nope = (w[:, :, :QK_NOPE] * Q_SCALE).reshape(Q_LORA, MLA_HEADS * QK_NOPE)
    rope = jnp.pad(w[:, :, QK_NOPE:], ((0, 0), (0, 0), (0, LANES - QK_ROPE))).reshape(Q_LORA, MLA_HEADS * LANES)
    return jnp.concatenate([nope, rope], axis=1).astype(BF16)


def kernel(x, mem, positions, mix_norm, w_in, q_norm, w_uq, kv_norm, w_ukv, w_o_mla, conv_w, w_out_conv,
           mem_norm, w_mem_kv, w_o_mem, gate_bias, w_o, ffn_norm, w_up, ffn_conv_w, w_down, final_norm):
    B, S, D = x.shape
    M = mem.shape[1]
    T = B * S
    depth = w_in.shape[0]
    x2 = x.reshape(T, D)
    mem2 = mem.reshape(B * M, D)
    pos = positions.reshape(T, 1)
    inv_freq = jnp.power(ROPE_THETA, -jnp.arange(0, QK_ROPE, 2, dtype=F32) / QK_ROPE)
    invf = jnp.concatenate([inv_freq, inv_freq, jnp.zeros((LANES - QK_ROPE,), F32)]).reshape(1, LANES)

    for l in range(depth):
        w_blk, w_kr = _pack_w_in(jnp.swapaxes(w_in[l], 0, 1))
        z, kr, w_up_blk, b_down = _in_proj(x2, mix_norm[l].reshape(1, D), w_blk, w_kr, w_up[l], w_down[l])
        q, k, v, (b_mem_kv, b_o_mla, b_out_conv, b_o_mem, b_o) = _qkv(
            z, kr, pos, invf, q_norm[l].reshape(1, -1), kv_norm[l].reshape(1, -1),
            _pack_w_uq(w_uq[l]), w_ukv[l].astype(BF16),
            (w_mem_kv[l], w_o_mla[l], w_out_conv[l], w_o_mem[l], w_o[l]))
        o_mla = _mla(q, k, v, batch=B, seq=S)
        kvm = _mem_kv(mem2, mem_norm[l].reshape(1, D), b_mem_kv)
        merged = _merge(o_mla, z, kvm, gate_bias[l].reshape(1, -1), conv_w[l], b_o_mla, b_out_conv, b_o_mem,
                        seq=S, mem_len=M)
        x2 = _out_proj(merged, b_o, x2)
        x2 = _ffn(x2, ffn_norm[l].reshape(1, D), w_up_blk, _block_conv_taps(ffn_conv_w[l]),
                  b_down, final_norm.reshape(1, D), seq=S, final=(l == depth - 1))
    return x2.reshape(B, S, D)
```

```python
import functools

import jax
import jax.numpy as jnp
from jax import lax
from jax.experimental import pallas as pl
from jax.experimental.pallas import tpu as pltpu

F32 = jnp.float32
BF16 = jnp.bfloat16

D_MODEL = 2048
MLA_HEADS = 8
Q_LORA = 512
KV_LORA = 512
QK_NOPE = 128
QK_ROPE = 64
V_HEAD = 128
ROPE_THETA = 10000.0
CONV_WIDTH = 1024
X_HEADS = 4
X_HEAD_DIM = 256
X_WIDTH = X_HEADS * X_HEAD_DIM
D_FF = 5632
N_BRANCH = 3
EPS = 1e-6
LOG2_E = 1.4426950408889634
Q_SCALE = (QK_NOPE + QK_ROPE) ** -0.5 * LOG2_E

LANES = 128
SUBLANES_F32 = 8
SUBLANES_BF16 = 16
HEAD_PAD = 256
VMEM_LIMIT = 56 * 1024 * 1024

Z_G = 0
Z_CQ = Z_G + N_BRANCH * D_MODEL
Z_CKV = Z_CQ + Q_LORA
Z_CV = Z_CKV + KV_LORA
Z_CB = Z_CV + CONV_WIDTH
Z_CC = Z_CB + CONV_WIDTH
Z_QX = Z_CC + CONV_WIDTH
Z_COLS = Z_QX + X_WIDTH


_NT = (((1,), (1,)), ((), ()))


def _params(*sem):
    return pltpu.CompilerParams(dimension_semantics=sem, vmem_limit_bytes=VMEM_LIMIT)


def _rms(x, g):
    inv = lax.rsqrt(jnp.mean(x * x, axis=-1, keepdims=True) + EPS)
    return x * inv * g


def _rms_rows_to(x_ref, g_ref, out_ref, rows, chunk, out_row0=0):
    def body(c, carry):
        r0 = pl.multiple_of(c * chunk, chunk)
        x = x_ref[pl.ds(r0, chunk), :].astype(F32)
        out_ref[pl.ds(out_row0 + r0, chunk), :] = _rms(x, g_ref[...]).astype(out_ref.dtype)
        return carry
    lax.fori_loop(0, rows // chunk, body, 0)


IN_TN = 1024
FFN_TF = 512


N_GATE_BLK = N_BRANCH * D_MODEL // IN_TN


def _pack_w_in_kernel(a_ref, b_ref, kr_ref, wblk_ref, wkr_ref):
    j = pl.program_id(0)

    @pl.when(j == N_GATE_BLK)
    def _():
        wblk_ref[0] = a_ref[...].astype(BF16)

    @pl.when(j != N_GATE_BLK)
    def _():
        wblk_ref[0, :IN_TN - QK_ROPE, :] = a_ref[QK_ROPE:, :].astype(BF16)
        wblk_ref[0, IN_TN - QK_ROPE:, :] = b_ref[...].astype(BF16)

    @pl.when(j == 0)
    def _():
        wkr_ref[:QK_ROPE, :] = kr_ref[...].astype(BF16)
        wkr_ref[QK_ROPE:, :] = jnp.zeros((LANES - QK_ROPE, wkr_ref.shape[1]), BF16)


def _pack_w_in(wi_t):
    cols, d = wi_t.shape
    kr0 = Q_LORA + KV_LORA
    assert cols == Z_COLS + QK_ROPE and kr0 == IN_TN and IN_TN % QK_ROPE == 0
    n_blk = Z_COLS // IN_TN
    n_tail = n_blk - 1 - N_GATE_BLK
    sub = IN_TN // QK_ROPE

    def src_blk(j):
        return jnp.where(j < N_GATE_BLK, j + 1 + n_tail, jnp.where(j == N_GATE_BLK, 0, j - N_GATE_BLK))

    return pl.pallas_call(
        _pack_w_in_kernel,
        grid=(n_blk,),
        in_specs=[
            pl.BlockSpec((IN_TN, d), lambda j: (src_blk(j), 0)),
            pl.BlockSpec((QK_ROPE, d), lambda j: ((src_blk(j) + 1) * sub, 0)),
            pl.BlockSpec((QK_ROPE, d), lambda j: (kr0 // QK_ROPE, 0)),
        ],
        out_specs=[
            pl.BlockSpec((1, IN_TN, d), lambda j: (j, 0, 0)),
            pl.BlockSpec((LANES, d), lambda j: (0, 0)),
        ],
        out_shape=[
            jax.ShapeDtypeStruct((n_blk, IN_TN, d), BF16),
            jax.ShapeDtypeStruct((LANES, d), BF16),
        ],
        compiler_params=_params("arbitrary"),
        name="pack_w_in",
    )(wi_t, wi_t, wi_t)


def _in_proj_kernel(x_ref, g_ref, w_ref, wkr_ref, wu_ref, wd_ref, z_ref, kr_ref, wub_ref, wdb_ref, h_ref, *, tm):
    @pl.when(pl.program_id(1) == 0)
    def _():
        _rms_rows_to(x_ref, g_ref, h_ref, tm, 128)
        kr_ref[...] = lax.dot_general(h_ref[...], wkr_ref[...], _NT, preferred_element_type=F32)

    z_ref[...] = lax.dot_general(h_ref[...], w_ref[0], _NT, preferred_element_type=F32).astype(BF16)
    wub_ref[0] = wu_ref[...].astype(BF16)
    wdb_ref[...] = wd_ref[...].astype(BF16)


def _in_proj(x2, g, w_blk, w_kr, w_up, w_down, *, tm=1024):
    T = x2.shape[0]
    tn = IN_TN
    n_i, n_j = T // tm, Z_COLS // tn
    n_steps = n_i * n_j
    d = w_up.shape[0]
    n_f = D_FF // FFN_TF
    up_rows = d * 2 * n_f // n_steps
    down_rows = w_down.shape[0] // n_steps
    assert up_rows * n_steps == d * 2 * n_f and d % up_rows == 0 and up_rows % SUBLANES_BF16 == 0
    assert down_rows * n_steps == w_down.shape[0] and down_rows % SUBLANES_BF16 == 0
    r_per_col = d // up_rows

    def up_in(i, j):
        step = i * n_j + j
        return (step % r_per_col, step // r_per_col)

    def up_out(i, j):
        step = i * n_j + j
        col = step // r_per_col
        return (col % n_f, step % r_per_col, col // n_f)

    return pl.pallas_call(
        functools.partial(_in_proj_kernel, tm=tm),
        grid=(n_i, n_j),
        in_specs=[
            pl.BlockSpec((tm, D_MODEL), lambda i, j: (i, 0)),
            pl.BlockSpec((1, D_MODEL), lambda i, j: (0, 0)),
            pl.BlockSpec((1, tn, D_MODEL), lambda i, j: (j, 0, 0)),
            pl.BlockSpec((LANES, D_MODEL), lambda i, j: (0, 0)),
            pl.BlockSpec((up_rows, FFN_TF), up_in),
            pl.BlockSpec((down_rows, D_MODEL), lambda i, j: (i * n_j + j, 0)),
        ],
        out_specs=[
            pl.BlockSpec((tm, tn), lambda i, j: (i, j)),
            pl.BlockSpec((tm, LANES), lambda i, j: (i, 0)),
            pl.BlockSpec((1, up_rows, FFN_TF), up_out),
            pl.BlockSpec((down_rows, D_MODEL), lambda i, j: (i * n_j + j, 0)),
        ],
        out_shape=[
            jax.ShapeDtypeStruct((T, Z_COLS), BF16),
            jax.ShapeDtypeStruct((T, LANES), F32),
            jax.ShapeDtypeStruct((n_f, d, 2 * FFN_TF), BF16),
            jax.ShapeDtypeStruct(w_down.shape, BF16),
        ],
        scratch_shapes=[pltpu.VMEM((tm, D_MODEL), BF16)],
        compiler_params=_params("arbitrary", "arbitrary"),
        name="in_proj",
    )(x2, g, w_blk, w_kr, w_up, w_down)


def _qkv_kernel(cq_ref, ckv_ref, kr_ref, pos_ref, invf_ref, qn_ref, kvn_ref, wuq_ref, wukv_ref, *rest, tm):
    n_w = (len(rest) - 3) // 2
    q_ref, k_ref, v_ref = rest[n_w:n_w + 3]
    for w_ref, b_ref in zip(rest[:n_w], rest[n_w + 3:]):
        b_ref[...] = w_ref[...].astype(BF16)

    ang = pos_ref[...].astype(F32) * invf_ref[...]
    cos = jnp.cos(ang)
    sin = jnp.sin(ang)
    lane = lax.broadcasted_iota(jnp.int32, (tm, LANES), 1)
    half = QK_ROPE // 2
    c_tab = jnp.where(lane < QK_ROPE, cos, 0.0)
    s_lo = jnp.where(lane < half, -sin, 0.0)
    s_hi = jnp.where((lane >= half) & (lane < QK_ROPE), sin, 0.0)

    def rope(t, tabs):
        c, lo, hi = tabs
        return t * c + pltpu.roll(t, LANES - half, 1) * lo + pltpu.roll(t, half, 1) * hi

    k_tabs = (c_tab, s_lo, s_hi)
    q_tabs = tuple(t * Q_SCALE for t in k_tabs)
    cqn = _rms(cq_ref[...].astype(F32), qn_ref[...]).astype(BF16)
    q = jnp.dot(cqn, wuq_ref[...], preferred_element_type=F32)
    nope_cols = MLA_HEADS * QK_NOPE
    for h in range(MLA_HEADS):
        q_ref[:, h * HEAD_PAD:h * HEAD_PAD + QK_NOPE] = q[:, h * QK_NOPE:(h + 1) * QK_NOPE].astype(BF16)
        q_ref[:, h * HEAD_PAD + QK_NOPE:(h + 1) * HEAD_PAD] = (
            rope(q[:, nope_cols + h * LANES:nope_cols + (h + 1) * LANES], q_tabs)).astype(BF16)

    ckvn = _rms(ckv_ref[...].astype(F32), kvn_ref[...]).astype(BF16)
    kv = jnp.dot(ckvn, wukv_ref[...], preferred_element_type=F32)
    k_rope = rope(kr_ref[...], k_tabs).astype(BF16)
    ones = jnp.ones((tm, V_HEAD), BF16)
    for h in range(MLA_HEADS):
        k_ref[:, h * HEAD_PAD:h * HEAD_PAD + QK_NOPE] = kv[:, h * HEAD_PAD:h * HEAD_PAD + QK_NOPE].astype(BF16)
        k_ref[:, h * HEAD_PAD + QK_NOPE:(h + 1) * HEAD_PAD] = k_rope
        v_ref[:, 2 * h * V_HEAD:(2 * h + 1) * V_HEAD] = kv[:, h * HEAD_PAD + QK_NOPE:(h + 1) * HEAD_PAD].astype(BF16)
        v_ref[:, (2 * h + 1) * V_HEAD:(2 * h + 2) * V_HEAD] = ones


def _qkv(z, kr, pos, invf, q_norm, kv_norm, w_uq_p, w_ukv, later_ws, *, tm=512):
    T = z.shape[0]
    qk_cols = MLA_HEADS * HEAD_PAD
    n_steps = T // tm
    assert all(w.shape[0] % (n_steps * SUBLANES_BF16) == 0 for w in later_ws)

    def slice_spec(w):
        return pl.BlockSpec((w.shape[0] // n_steps, w.shape[1]), lambda i: (i, 0))

    outs = pl.pallas_call(
        functools.partial(_qkv_kernel, tm=tm),
        grid=(n_steps,),
        in_specs=[
            pl.BlockSpec((tm, Q_LORA), lambda i: (i, Z_CQ // Q_LORA)),
            pl.BlockSpec((tm, KV_LORA), lambda i: (i, Z_CKV // KV_LORA)),
            pl.BlockSpec((tm, LANES), lambda i: (i, 0)),
            pl.BlockSpec((tm, 1), lambda i: (i, 0)),
            pl.BlockSpec((1, LANES), lambda i: (0, 0)),
            pl.BlockSpec((1, Q_LORA), lambda i: (0, 0)),
            pl.BlockSpec((1, KV_LORA), lambda i: (0, 0)),
            pl.BlockSpec((Q_LORA, qk_cols), lambda i: (0, 0)),
            pl.BlockSpec((KV_LORA, qk_cols), lambda i: (0, 0)),
        ] + [slice_spec(w) for w in later_ws],
        out_specs=[
            pl.BlockSpec((tm, qk_cols), lambda i: (i, 0)),
            pl.BlockSpec((tm, qk_cols), lambda i: (i, 0)),
            pl.BlockSpec((tm, 2 * MLA_HEADS * V_HEAD), lambda i: (i, 0)),
        ] + [slice_spec(w) for w in later_ws],
        out_shape=[
            jax.ShapeDtypeStruct((T, qk_cols), BF16),
            jax.ShapeDtypeStruct((T, qk_cols), BF16),
            jax.ShapeDtypeStruct((T, 2 * MLA_HEADS * V_HEAD), BF16),
        ] + [jax.ShapeDtypeStruct(w.shape, BF16) for w in later_ws],
        compiler_params=_params("arbitrary"),
        name="qkv",
    )(z, z, kr, pos, invf, q_norm, kv_norm, w_uq_p, w_ukv, *later_ws)
    return outs[0], outs[1], outs[2], outs[3:]


def _mla_kernel(q_ref, k_ref, v_ref, o_ref, *, seq, tq, heads):
    for h in range(heads):
        k = k_ref[:, h * HEAD_PAD:(h + 1) * HEAD_PAD]
        v = v_ref[:, 2 * h * V_HEAD:2 * (h + 1) * V_HEAD]
        for c in range(seq // tq):
            r = slice(c * tq, (c + 1) * tq)
            s = lax.dot_general(q_ref[r, h * HEAD_PAD:(h + 1) * HEAD_PAD], k, (((1,), (1,)), ((), ())),
                                preferred_element_type=F32)
            m = jnp.max(s, axis=-1, keepdims=True)
            p = jnp.exp2(s - m).astype(BF16)
            ov = jnp.dot(p, v, preferred_element_type=F32)
            o_ref[r, h * V_HEAD:(h + 1) * V_HEAD] = (ov[:, :V_HEAD] / ov[:, V_HEAD:]).astype(BF16)


def _mla(q, k, v, *, batch, seq, tq=512, heads=4):
    T = q.shape[0]
    return pl.pallas_call(
        functools.partial(_mla_kernel, seq=seq, tq=tq, heads=heads),
        grid=(batch, MLA_HEADS // heads),
        in_specs=[
            pl.BlockSpec((seq, heads * HEAD_PAD), lambda b, h: (b, h)),
            pl.BlockSpec((seq, heads * HEAD_PAD), lambda b, h: (b, h)),
            pl.BlockSpec((seq, heads * 2 * V_HEAD), lambda b, h: (b, h)),
        ],
        out_specs=pl.BlockSpec((seq, heads * V_HEAD), lambda b, h: (b, h)),
        out_shape=jax.ShapeDtypeStruct((T, MLA_HEADS * V_HEAD), BF16),
        compiler_params=_params("parallel", "parallel"),
        name="mla_attn",
    )(q, k, v)


def _mem_kv_kernel(m_ref, g_ref, w_ref, o_ref, h_ref, *, tm):
    @pl.when(pl.program_id(1) == 0)
    def _():
        _rms_rows_to(m_ref, g_ref, h_ref, tm, 128)

    o_ref[...] = jnp.dot(h_ref[...], w_ref[...], preferred_element_type=F32).astype(BF16)


def _mem_kv(mem2, g, w, *, tm=1024, tn=1024):
    R = mem2.shape[0]
    N = w.shape[1]
    return pl.pallas_call(
        functools.partial(_mem_kv_kernel, tm=tm),
        grid=(R // tm, N // tn),
        in_specs=[
            pl.BlockSpec((tm, D_MODEL), lambda i, j: (i, 0)),
            pl.BlockSpec((1, D_MODEL), lambda i, j: (0, 0)),
            pl.BlockSpec((D_MODEL, tn), lambda i, j: (0, j)),
        ],
        out_specs=pl.BlockSpec((tm, tn), lambda i, j: (i, j)),
        out_shape=jax.ShapeDtypeStruct((R, N), BF16),
        scratch_shapes=[pltpu.VMEM((tm, D_MODEL), BF16)],
        compiler_params=_params("parallel", "arbitrary"),
        name="mem_kv",
    )(mem2, g, w)


def _mem_attn_kernel(q_ref, kv_ref, o_ref):
    scale = X_HEAD_DIM ** -0.5
    for h in range(X_HEADS):
        c = slice(h * X_HEAD_DIM, (h + 1) * X_HEAD_DIM)
        k = kv_ref[:, h * X_HEAD_DIM:(h + 1) * X_HEAD_DIM]
        v = kv_ref[:, X_WIDTH + h * X_HEAD_DIM:X_WIDTH + (h + 1) * X_HEAD_DIM]
        s = lax.dot_general(q_ref[:, c], k, (((1,), (1,)), ((), ())), preferred_element_type=F32) * scale
        m = jnp.max(s, axis=-1, keepdims=True)
        p = jnp.exp(s - m)
        l = jnp.sum(p, axis=-1, keepdims=True)
        o = jnp.dot(p.astype(BF16), v, preferred_element_type=F32)
        o_ref[:, c] = (o / l).astype(BF16)


def _mem_attn(z, kvm, *, seq, mem_len, tm=1024):
    T = z.shape[0]
    return pl.pallas_call(
        _mem_attn_kernel,
        grid=(T // tm,),
        in_specs=[
            pl.BlockSpec((tm, X_WIDTH), lambda i: (i, Z_QX // X_WIDTH)),
            pl.BlockSpec((mem_len, 2 * X_WIDTH), lambda i: (i // (seq // tm), 0)),
        ],
        out_specs=pl.BlockSpec((tm, X_WIDTH), lambda i: (i, 0)),
        out_shape=jax.ShapeDtypeStruct((T, X_WIDTH), BF16),
        compiler_params=_params("parallel"),
        name="mem_attn",
    )(z, kvm)


def _merge_kernel(o_ref, cv_ref, cb_ref, cc_ref, cvp_ref, ccp_ref, cvn_ref, ccn_ref, qx_ref, kvm_ref,
                  g0_ref, g1_ref, g2_ref, b_ref, cw_ref,
                  wa_ref, wc_ref, wm_ref, out_ref, pext_ref, *, tm, tn, tiles_per_seq):
    heads = []
    for h in range(X_HEADS):
        hc = slice(h * X_HEAD_DIM, (h + 1) * X_HEAD_DIM)
        k = kvm_ref[:, h * X_HEAD_DIM:(h + 1) * X_HEAD_DIM]
        v = kvm_ref[:, X_WIDTH + h * X_HEAD_DIM:X_WIDTH + (h + 1) * X_HEAD_DIM]
        s = lax.dot_general(qx_ref[:, hc], k, _NT, preferred_element_type=F32) * (X_HEAD_DIM ** -0.5)
        p = jnp.exp(s - jnp.max(s, axis=-1, keepdims=True))
        l = jnp.sum(p, axis=-1, keepdims=True)
        heads.append((jnp.dot(p.astype(BF16), v, preferred_element_type=F32) / l).astype(BF16))
    yx = jnp.concatenate(heads, axis=1)

    pos_in_seq = pl.program_id(0) % tiles_per_seq
    keep_prev = jnp.where(pos_in_seq == 0, 0.0, 1.0)
    keep_next = jnp.where(pos_in_seq == tiles_per_seq - 1, 0.0, 1.0)
    h8 = SUBLANES_F32
    prev = (cvp_ref[h8:, :].astype(F32) * ccp_ref[h8:, :].astype(F32)) * keep_prev
    nxt = (cvn_ref[:h8, :].astype(F32) * ccn_ref[:h8, :].astype(F32)) * keep_next
    pext_ref[0:h8, :] = prev
    pext_ref[tm + h8:tm + 2 * h8, :] = nxt
    pext_ref[h8:tm + h8, :] = cv_ref[...].astype(F32) * cc_ref[...].astype(F32)
    conv = (cw_ref[0:1, :] * pext_ref[h8 - 1:tm + h8 - 1, :] + cw_ref[1:2, :] * pext_ref[h8:tm + h8, :]
            + cw_ref[2:3, :] * pext_ref[h8 + 1:tm + h8 + 1, :])
    cbr = (cb_ref[...].astype(F32) * conv).astype(BF16)

    def gate(g_ref, k, c):
        bias = b_ref[:, k * D_MODEL + c.start:k * D_MODEL + c.stop].astype(BF16)
        return jax.nn.sigmoid(g_ref[:, c] + bias).astype(F32)

    for n in range(D_MODEL // tn):
        c = slice(n * tn, (n + 1) * tn)
        y = gate(g0_ref, 0, c) * jnp.dot(o_ref[...], wa_ref[:, c], preferred_element_type=F32)
        y += gate(g1_ref, 1, c) * jnp.dot(cbr, wc_ref[:, c], preferred_element_type=F32)
        y += gate(g2_ref, 2, c) * jnp.dot(yx, wm_ref[:, c], preferred_element_type=F32)
        out_ref[:, c] = y.astype(BF16)


def _merge(o_mla, z, kvm, gate_bias, conv_w, w_o_mla, w_out_conv, w_o_mem, *, seq, mem_len, tm=512, tn=2048):
    T = z.shape[0]
    hb = SUBLANES_BF16
    rb = tm // hb
    last_hb = T // hb - 1
    cw = CONV_WIDTH
    once = pl.Buffered(1)

    def zcol(c0, width):
        return c0 // width

    def gate_spec(k):
        return pl.BlockSpec((tm, D_MODEL), lambda i, k=k: (i, Z_G // D_MODEL + k))

    def prev_spec(c0):
        return pl.BlockSpec((hb, cw), lambda i: (jnp.maximum(i * rb - 1, 0), zcol(c0, cw)))

    def next_spec(c0):
        return pl.BlockSpec((hb, cw), lambda i: (jnp.minimum((i + 1) * rb, last_hb), zcol(c0, cw)))

    def w_spec():
        return pl.BlockSpec((cw, D_MODEL), lambda i: (0, 0), pipeline_mode=once)

    assert Z_G % D_MODEL == 0
    return pl.pallas_call(
        functools.partial(_merge_kernel, tm=tm, tn=tn, tiles_per_seq=seq // tm),
        grid=(T // tm,),
        in_specs=[
            pl.BlockSpec((tm, MLA_HEADS * V_HEAD), lambda i: (i, 0)),
            pl.BlockSpec((tm, cw), lambda i: (i, zcol(Z_CV, cw))),
            pl.BlockSpec((tm, cw), lambda i: (i, zcol(Z_CB, cw))),
            pl.BlockSpec((tm, cw), lambda i: (i, zcol(Z_CC, cw))),
            prev_spec(Z_CV), prev_spec(Z_CC), next_spec(Z_CV), next_spec(Z_CC),
            pl.BlockSpec((tm, X_WIDTH), lambda i: (i, Z_QX // X_WIDTH)),
            pl.BlockSpec((mem_len, 2 * X_WIDTH), lambda i: (i // (seq // tm), 0)),
            gate_spec(0), gate_spec(1), gate_spec(2),
            pl.BlockSpec((1, N_BRANCH * D_MODEL), lambda i: (0, 0), pipeline_mode=once),
            pl.BlockSpec((3, cw), lambda i: (0, 0), pipeline_mode=once),
            w_spec(), w_spec(), w_spec(),
        ],
        out_specs=pl.BlockSpec((tm, D_MODEL), lambda i: (i, 0)),
        out_shape=jax.ShapeDtypeStruct((T, D_MODEL), BF16),
        scratch_shapes=[pltpu.VMEM((tm + 2 * SUBLANES_F32, cw), F32)],
        compiler_params=_params("parallel"),
        name="merge",
    )(o_mla, z, z, z, z, z, z, z, z, kvm, z, z, z, gate_bias, conv_w, w_o_mla, w_out_conv, w_o_mem)


def _out_proj_kernel(m_ref, w_ref, x_ref, o_ref):
    o_ref[...] = x_ref[...] + jnp.dot(m_ref[...], w_ref[...], preferred_element_type=F32)


def _out_proj(merged, w_o, x2, *, tm=512):
    T = x2.shape[0]
    return pl.pallas_call(
        _out_proj_kernel,
        grid=(T // tm,),
        in_specs=[
            pl.BlockSpec((tm, D_MODEL), lambda i: (i, 0)),
            pl.BlockSpec((D_MODEL, D_MODEL), lambda i: (0, 0), pipeline_mode=pl.Buffered(1)),
            pl.BlockSpec((tm, D_MODEL), lambda i: (i, 0)),
        ],
        out_specs=pl.BlockSpec((tm, D_MODEL), lambda i: (i, 0)),
        out_shape=jax.ShapeDtypeStruct((T, D_MODEL), F32),
        compiler_params=_params("parallel"),
        name="out_proj",
    )(merged, w_o, x2)


FFN_COL_CHUNK = 256


def _ffn_kernel(x_ref, xp_ref, xn_ref, g_ref, wu_ref, ct_ref, wd_ref, fg_ref,
                o_ref, hn_ref, u_ref, *, tm, tf, tiles_per_seq, n_f, bps, final):
    acc_ref = o_ref
    i = pl.program_id(0)
    s = pl.program_id(1)
    h8 = SUBLANES_F32
    cw = FFN_COL_CHUNK
    rows = 128
    n_chunks = tf // cw
    n_full = n_f // bps
    n_steps = pl.cdiv(n_f, bps)

    @pl.when(s == 0)
    def _():
        pos_in_seq = i % tiles_per_seq
        keep_prev = jnp.where(pos_in_seq == 0, 0.0, 1.0)
        keep_next = jnp.where(pos_in_seq == tiles_per_seq - 1, 0.0, 1.0)
        halo = jnp.concatenate([_rms(xp_ref[...], g_ref[...]) * keep_prev,
                                _rms(xn_ref[...], g_ref[...]) * keep_next], axis=0)
        hn_ref[tm:tm + 2 * h8, :] = halo.astype(BF16)
        _rms_rows_to(x_ref, g_ref, hn_ref, tm, rows)
        acc_ref[...] = jnp.zeros_like(acc_ref)

    def up(fb, u_ref):
        r = jnp.dot(hn_ref[...], wu_ref[fb], preferred_element_type=F32)
        u_ref[h8:tm + h8, :] = r[:tm]
        u_ref[0:h8, :] = r[tm:tm + h8]
        u_ref[tm + h8:tm + 2 * h8, :] = r[tm + h8:]

    def conv(u_ref, fb, c0):
        return (ct_ref[fb, 0:1, c0:c0 + cw] * u_ref[h8 - 1:tm + h8 - 1, c0:c0 + cw]
                + ct_ref[fb, 1:2, c0:c0 + cw] * u_ref[h8:tm + h8, c0:c0 + cw]
                + ct_ref[fb, 2:3, c0:c0 + cw] * u_ref[h8 + 1:tm + h8 + 1, c0:c0 + cw])

    def block(fb):
        up(fb, u_ref.at[fb])
        down = None
        for n in range(n_chunks):
            c0 = n * cw
            a = conv(u_ref.at[fb], fb, c0)
            b = conv(u_ref.at[fb], fb, tf + c0)
            act = (a * jax.nn.sigmoid(a) * b).astype(BF16)
            d = jnp.dot(act, wd_ref[fb * tf + c0:fb * tf + c0 + cw, :], preferred_element_type=F32)
            down = d if down is None else down + d
        acc_ref[...] += down

    def blocks(count):
        for fb in range(count):
            block(fb)

    pl.when(s < n_full)(functools.partial(blocks, bps))
    if n_f % bps:
        pl.when(s == n_full)(functools.partial(blocks, n_f % bps))

    @pl.when(s == n_steps - 1)
    def _():
        def body(c, carry):
            r = pl.ds(pl.multiple_of(c * rows, rows), rows)
            y = x_ref[r, :] + acc_ref[r, :]
            o_ref[r, :] = _rms(y, fg_ref[...]) if final else y
            return carry
        lax.fori_loop(0, tm // rows, body, 0)


def _ffn(x1, ffn_norm, w_up_blk, conv_taps, w_down, final_norm, *, seq, final, tm=512, bps=2):
    T = x1.shape[0]
    h8 = SUBLANES_F32
    rb = tm // h8
    last_hb = T // h8 - 1
    n_f = w_up_blk.shape[0]
    tf = w_up_blk.shape[2] // 2
    return pl.pallas_call(
        functools.partial(_ffn_kernel, tm=tm, tf=tf, tiles_per_seq=seq // tm, n_f=n_f, bps=bps, final=final),
        grid=(T // tm, pl.cdiv(n_f, bps)),
        in_specs=[
            pl.BlockSpec((tm, D_MODEL), lambda i, s: (i, 0)),
            pl.BlockSpec((h8, D_MODEL), lambda i, s: (jnp.maximum(i * rb - 1, 0), 0)),
            pl.BlockSpec((h8, D_MODEL), lambda i, s: (jnp.minimum((i + 1) * rb, last_hb), 0)),
            pl.BlockSpec((1, D_MODEL), lambda i, s: (0, 0)),
            pl.BlockSpec((bps, D_MODEL, 2 * tf), lambda i, s: (s, 0, 0)),
            pl.BlockSpec((bps, 3, 2 * tf), lambda i, s: (s, 0, 0)),
            pl.BlockSpec((bps * tf, D_MODEL), lambda i, s: (s, 0)),
            pl.BlockSpec((1, D_MODEL), lambda i, s: (0, 0)),
        ],
        out_specs=pl.BlockSpec((tm, D_MODEL), lambda i, s: (i, 0)),
        out_shape=jax.ShapeDtypeStruct((T, D_MODEL), F32),
        scratch_shapes=[
            pltpu.VMEM((tm + 2 * h8, D_MODEL), BF16),
            pltpu.VMEM((bps, tm + 2 * h8, 2 * tf), F32),
        ],
        compiler_params=_params("parallel", "arbitrary"),
        name="ffn",
    )(x1, x1, x1, ffn_norm, w_up_blk, conv_taps, w_down, final_norm)


def _block_conv_taps(ffn_conv_w):
    k = ffn_conv_w.shape[0]
    n_f = D_FF // FFN_TF
    halves = ffn_conv_w.reshape(k, 2, n_f, FFN_TF)
    return jnp.transpose(halves, (2, 0, 1, 3)).reshape(n_f, k, 2 * FFN_TF)


def _pack_w_uq(w_uq):
    w = w_uq.reshape(Q_LORA, MLA_HEADS, QK_NOPE + QK_ROPE)
    nope = (w[:, :, :QK_NOPE] * Q_SCALE).reshape(Q_LORA, MLA_HEADS * QK_NOPE)
    rope = jnp.pad(w[:, :, QK_NOPE:], ((0, 0), (0, 0), (0, LANES - QK_ROPE))).reshape(Q_LORA, MLA_HEADS * LANES)
    return jnp.concatenate([nope, rope], axis=1).astype(BF16)


def kernel(x, mem, positions, mix_norm, w_in, q_norm, w_uq, kv_norm, w_ukv, w_o_mla, conv_w, w_out_conv,
           mem_norm, w_mem_kv, w_o_mem, gate_bias, w_o, ffn_norm, w_up, ffn_conv_w, w_down, final_norm):
    B, S, D = x.shape
    M = mem.shape[1]
    T = B * S
    depth = w_in.shape[0]
    x2 = x.reshape(T, D)
    mem2 = mem.reshape(B * M, D)
    pos = positions.reshape(T, 1)
    inv_freq = jnp.power(ROPE_THETA, -jnp.arange(0, QK_ROPE, 2, dtype=F32) / QK_ROPE)
    invf = jnp.concatenate([inv_freq, inv_freq, jnp.zeros((LANES - QK_ROPE,), F32)]).reshape(1, LANES)

    for l in range(depth):
        w_blk, w_kr = _pack_w_in(jnp.swapaxes(w_in[l], 0, 1))
        z, kr, w_up_blk, b_down = _in_proj(x2, mix_norm[l].reshape(1, D), w_blk, w_kr, w_up[l], w_down[l])
        q, k, v, (b_mem_kv, b_o_mla, b_out_conv, b_o_mem, b_o) = _qkv(
            z, kr, pos, invf, q_norm[l].reshape(1, -1), kv_norm[l].reshape(1, -1),
            _pack_w_uq(w_uq[l]), w_ukv[l].astype(BF16),
            (w_mem_kv[l], w_o_mla[l], w_out_conv[l], w_o_mem[l], w_o[l]))
        o_mla = _mla(q, k, v, batch=B, seq=S)
        kvm = _mem_kv(mem2, mem_norm[l].reshape(1, D), b_mem_kv)
        merged = _merge(o_mla, z, kvm, gate_bias[l].reshape(1, -1), conv_w[l], b_o_mla, b_out_conv, b_o_mem,
                        seq=S, mem_len=M)
        x2 = _out_proj(merged, b_o, x2)
        x2 = _ffn(x2, ffn_norm[l].reshape(1, D), w_up_blk, _block_conv_taps(ffn_conv_w[l]),
                  b_down, final_norm.reshape(1, D), seq=S, final=(l == depth - 1))
    return x2.reshape(B, S, D)
```
